```python
import jax, jax.numpy as jnp
from jax import lax
import numpy as np

D_MODEL = 1024
BATCH = 2
SEQ = 16384
DEPTH = 4

N_A_LAYERS = DEPTH // 2
N_B_LAYERS = DEPTH - N_A_LAYERS

POOL_WINDOWS = (2, 4, 8, 16)
N_POOL_GROUPS = len(POOL_WINDOWS)
POOL_GROUP = D_MODEL // N_POOL_GROUPS

N_HEADS = 16
QK_NOPE_DIM = 128
QK_ROPE_DIM = 64
QK_DIM = QK_NOPE_DIM + QK_ROPE_DIM
V_DIM = 128
Q_LORA_RANK = 256
KV_LORA_RANK = 128
ROPE_THETA = 10000.0
Q_BLOCK = 128

D_FF = 2816
RMS_EPS = 1e-6

kernel_name = "yoco_pool_mla_macaron_trunk"


def rmsnorm(x, g):
    xf = x.astype(jnp.float32)
    y = xf * lax.rsqrt(jnp.mean(xf * xf, axis=-1, keepdims=True) + RMS_EPS)
    return (y * g.astype(jnp.float32)).astype(x.dtype)


def swiglu(u, wg, wu, wd):
    return (jax.nn.silu(u @ wg) * (u @ wu)) @ wd


def rope_tables(seq):
    pos = jnp.arange(seq, dtype=jnp.float32)
    inv_freq = ROPE_THETA ** (-jnp.arange(0, QK_ROPE_DIM, 2, dtype=jnp.float32) / QK_ROPE_DIM)
    ang = pos[:, None] * inv_freq[None, :]
    return jnp.cos(ang), jnp.sin(ang)


def apply_rope(x, cos, sin):
    half = x.shape[-1] // 2
    x1, x2 = x[..., :half], x[..., half:]
    c, s = cos.astype(x.dtype), sin.astype(x.dtype)
    return jnp.concatenate([x1 * c - x2 * s, x1 * s + x2 * c], axis=-1)


def pool_mixer(u, w_groups, scale):
    B, S, D = u.shape
    uf = u.astype(jnp.float32)
    c = jnp.cumsum(uf, axis=1)
    count = jnp.arange(1, S + 1, dtype=jnp.float32)[None, :, None]
    outs = []
    for g, w in enumerate(POOL_WINDOWS):
        sl = slice(g * POOL_GROUP, (g + 1) * POOL_GROUP)
        cg = c[..., sl]
        lag = jnp.pad(cg, ((0, 0), (w, 0), (0, 0)))[:, :S]
        mean = (cg - lag) / jnp.minimum(count, float(w))
        outs.append(mean - uf[..., sl])
    y = jnp.stack(outs, axis=2).astype(u.dtype)
    z = jnp.einsum('bsgc,gcd->bsgd', y, w_groups).reshape(B, S, D)
    return z * scale


def mla_shared_kv(h, kv_in_norm, w_dkv, ckv_norm, w_uk, w_uv, cos, sin):
    u = rmsnorm(h, kv_in_norm)
    kv_a = u @ w_dkv
    c_kv = rmsnorm(kv_a[..., :KV_LORA_RANK], ckv_norm)
    k_rope = apply_rope(kv_a[..., KV_LORA_RANK:], cos[None], sin[None])
    k_nope = jnp.einsum('bsr,rhd->bshd', c_kv, w_uk)
    v = jnp.einsum('bsr,rhd->bshd', c_kv, w_uv)
    return k_nope, k_rope, v


def mla_attention(u, q_lora_norm, w_dq, w_uq, w_o, k_nope, k_rope, v, cos, sin):
    B, S, _ = u.shape
    cq = rmsnorm(u @ w_dq, q_lora_norm)
    q = jnp.einsum('bsr,rhd->bshd', cq, w_uq)
    q_nope = q[..., :QK_NOPE_DIM]
    q_rope = apply_rope(q[..., QK_NOPE_DIM:], cos[None, :, None], sin[None, :, None])
    nblk = S // Q_BLOCK
    qn = q_nope.reshape(B, nblk, Q_BLOCK, N_HEADS, QK_NOPE_DIM).transpose(1, 0, 2, 3, 4)
    qr = q_rope.reshape(B, nblk, Q_BLOCK, N_HEADS, QK_ROPE_DIM).transpose(1, 0, 2, 3, 4)
    scale = QK_DIM ** -0.5
    key_pos = jnp.arange(S)

    def block(args):
        i, qn_b, qr_b = args
        s = (jnp.einsum('bqhd,bkhd->bhqk', qn_b, k_nope)
             + jnp.einsum('bqhr,bkr->bhqk', qr_b, k_rope))
        s = s.astype(jnp.float32) * scale
        q_pos = i * Q_BLOCK + jnp.arange(Q_BLOCK)
        mask = key_pos[None, :] <= q_pos[:, None]
        s = jnp.where(mask[None, None], s, -jnp.inf)
        p = jax.nn.softmax(s, axis=-1).astype(v.dtype)
        return jnp.einsum('bhqk,bkhd->bqhd', p, v)

    o = lax.map(block, (jnp.arange(nblk), qn, qr))
    o = o.transpose(1, 0, 2, 3, 4).reshape(B, S, N_HEADS * V_DIM)
    return o @ w_o


def setup_inputs(seed: int = 0) -> dict:
    key = jax.random.key(seed)
    ks = jax.random.split(key, 32)
    f32 = jnp.float32

    def w(k, shape, fan_in):
        return jax.random.normal(k, shape, f32) * (fan_in ** -0.5)

    def gain(k, shape):
        return 1.0 + 0.02 * jax.random.normal(k, shape, f32)

    D, F, G, Dg = D_MODEL, D_FF, N_POOL_GROUPS, POOL_GROUP
    return {
        "x": jax.random.normal(ks[0], (BATCH, SEQ, D), f32),
        "ffn_pre_norm": gain(ks[1], (DEPTH, D)),
        "ffn_pre_wg": w(ks[2], (DEPTH, D, F), D),
        "ffn_pre_wu": w(ks[3], (DEPTH, D, F), D),
        "ffn_pre_wd": w(ks[4], (DEPTH, F, D), F),
        "mix_norm": gain(ks[5], (DEPTH, D)),
        "ffn_post_norm": gain(ks[6], (DEPTH, D)),
        "ffn_post_wg": w(ks[7], (DEPTH, D, F), D),
        "ffn_post_wu": w(ks[8], (DEPTH, D, F), D),
        "ffn_post_wd": w(ks[9], (DEPTH, F, D), F),
        "pool_w": w(ks[10], (N_A_LAYERS, G, Dg, Dg), Dg),
        "pool_scale": gain(ks[11], (N_A_LAYERS, D)),
        "kv_in_norm": gain(ks[12], (D,)),
        "w_dkv": w(ks[13], (D, KV_LORA_RANK + QK_ROPE_DIM), D),
        "ckv_norm": gain(ks[14], (KV_LORA_RANK,)),
        "w_uk": w(ks[15], (KV_LORA_RANK, N_HEADS, QK_NOPE_DIM), KV_LORA_RANK),
        "w_uv": w(ks[16], (KV_LORA_RANK, N_HEADS, V_DIM), KV_LORA_RANK),
        "q_lora_norm": gain(ks[17], (N_B_LAYERS, Q_LORA_RANK)),
        "w_dq": w(ks[18], (N_B_LAYERS, D, Q_LORA_RANK), D),
        "w_uq": w(ks[19], (N_B_LAYERS, Q_LORA_RANK, N_HEADS, QK_DIM), Q_LORA_RANK),
        "w_o": w(ks[20], (N_B_LAYERS, N_HEADS * V_DIM, D), N_HEADS * V_DIM),
        "final_norm": gain(ks[21], (D,)),
    }


def reference(x, ffn_pre_norm, ffn_pre_wg, ffn_pre_wu, ffn_pre_wd, mix_norm,
              ffn_post_norm, ffn_post_wg, ffn_post_wu, ffn_post_wd,
              pool_w, pool_scale, kv_in_norm, w_dkv, ckv_norm, w_uk, w_uv,
              q_lora_norm, w_dq, w_uq, w_o, final_norm):
    S = x.shape[1]
    cos, sin = rope_tables(S)
    h = x
    k_nope = k_rope = v = None
    for l in range(DEPTH):
        h = h + 0.5 * swiglu(rmsnorm(h, ffn_pre_norm[l]), ffn_pre_wg[l], ffn_pre_wu[l], ffn_pre_wd[l])
        u = rmsnorm(h, mix_norm[l])
        if l < N_A_LAYERS:
            h = h + pool_mixer(u, pool_w[l], pool_scale[l])
        else:
            j = l - N_A_LAYERS
            h = h + mla_attention(u, q_lora_norm[j], w_dq[j], w_uq[j], w_o[j],
                                  k_nope, k_rope, v, cos, sin)
        h = h + 0.5 * swiglu(rmsnorm(h, ffn_post_norm[l]), ffn_post_wg[l], ffn_post_wu[l], ffn_post_wd[l])
        if l == N_A_LAYERS - 1:
            k_nope, k_rope, v = mla_shared_kv(h, kv_in_norm, w_dkv, ckv_norm, w_uk, w_uv, cos, sin)
    return rmsnorm(h, final_norm)
```

```python
import functools

import jax
import jax.numpy as jnp
from jax import lax
from jax.experimental import pallas as pl
from jax.experimental.pallas import tpu as pltpu

RMS_EPS = 1e-6
POOL_WINDOWS = (2, 4, 8, 16)
MAX_WINDOW = max(POOL_WINDOWS)
N_HEADS = 16
QK_NOPE_DIM = 128
QK_ROPE_DIM = 64
QK_DIM = QK_NOPE_DIM + QK_ROPE_DIM
V_DIM = 128
KV_LORA_RANK = 128
ROPE_THETA = 10000.0
HALF_ROPE = QK_ROPE_DIM // 2

VMEM_LIMIT_BYTES = 56 * 1024 * 1024
LANES = 128
QK_PAD = 256
MASK_VALUE = -1e30

BF16 = jnp.bfloat16
F32 = jnp.float32


def _rmsnorm(x, g):
    ms = jnp.mean(x * x, axis=-1, keepdims=True)
    return x * lax.rsqrt(ms + RMS_EPS) * g


def _const_spec(shape):
    nd = len(shape)
    return pl.BlockSpec(shape, lambda *_: (0,) * nd, pipeline_mode=pl.Buffered(1))


def _params(sem):
    return pltpu.CompilerParams(dimension_semantics=sem,
                                vmem_limit_bytes=VMEM_LIMIT_BYTES)


def _ffn_kernel(*refs, tf, with_oproj, with_final):
    it = iter(refs)
    h_ref = next(it)
    if with_oproj:
        o_ref = next(it)
        wo_ref = next(it)
    g_ref, wg_ref, wu_ref, wd_ref = next(it), next(it), next(it), next(it)
    if with_final:
        fg_ref = next(it)
    out_ref = next(it)
    a_ref = next(it)

    x = h_ref[...]
    if with_oproj:
        x = x + jnp.dot(o_ref[...], wo_ref[...], preferred_element_type=F32)
    u = _rmsnorm(x, g_ref[...]).astype(BF16)
    d_ff = wg_ref.shape[1]
    for c in range(d_ff // tf):
        sl = slice(c * tf, (c + 1) * tf)
        gate = jnp.dot(u, wg_ref[:, sl], preferred_element_type=F32)
        up = jnp.dot(u, wu_ref[:, sl], preferred_element_type=F32)
        a_ref[:, sl] = (gate * jax.nn.sigmoid(gate) * up).astype(BF16)
    y = jnp.dot(a_ref[...], wd_ref[...], preferred_element_type=F32)
    res = x + 0.5 * y
    if with_final:
        res = _rmsnorm(res, fg_ref[...])
    out_ref[...] = res


def _ffn(h2d, g, wg, wu, wd, *, o2d=None, wo=None, final_g=None, tm=512, tf=256):
    t, d = h2d.shape
    d_ff = wg.shape[1]
    tm = min(tm, t)
    row = lambda i: (i, 0)
    args = [h2d]
    specs = [pl.BlockSpec((tm, d), row)]
    if o2d is not None:
        args += [o2d, wo]
        specs += [pl.BlockSpec((tm, o2d.shape[1]), row), _const_spec(wo.shape)]
    args += [g.reshape(1, d), wg, wu, wd]
    specs += [_const_spec((1, d)), _const_spec(wg.shape), _const_spec(wu.shape),
              _const_spec(wd.shape)]
    if final_g is not None:
        args.append(final_g.reshape(1, d))
        specs.append(_const_spec((1, d)))
    kern = functools.partial(_ffn_kernel, tf=tf, with_oproj=o2d is not None,
                             with_final=final_g is not None)
    return pl.pallas_call(
        kern,
        grid=(t // tm,),
        in_specs=specs,
        out_specs=pl.BlockSpec((tm, d), row),
        out_shape=jax.ShapeDtypeStruct((t, d), F32),
        scratch_shapes=[pltpu.VMEM((tm, d_ff), BF16)],
        compiler_params=_params(("arbitrary",)),
        name="ffn",
    )(*args)


def _pool_kernel(h_ref, halo_ref, g_ref, w_ref, sc_ref, out_ref, ext_ref, *, ts):
    i = pl.program_id(1)
    g = g_ref[...]
    x = h_ref[0]
    u = _rmsnorm(x, g)
    halo = _rmsnorm(halo_ref[0], g)
    ext_ref[0:MAX_WINDOW, :] = jnp.where(i > 0, halo, 0.0)
    ext_ref[MAX_WINDOW:, :] = u
    pos = i * ts + lax.broadcasted_iota(jnp.int32, (ts, 1), 0)
    dg = w_ref.shape[1]
    for gi, w in enumerate(POOL_WINDOWS):
        cs = slice(gi * dg, (gi + 1) * dg)
        ug = ext_ref[MAX_WINDOW:MAX_WINDOW + ts, cs]
        win = ug
        for j in range(1, w):
            win = win + ext_ref[MAX_WINDOW - j:MAX_WINDOW - j + ts, cs]
        count = jnp.minimum(pos + 1, w).astype(F32)
        y = (win / count - ug).astype(BF16)
        z = jnp.dot(y, w_ref[gi], preferred_element_type=F32)
        out_ref[0, :, cs] = x[:, cs] + z * sc_ref[:, cs]


def _pool(h, g, w_groups, scale, *, ts=512):
    b, s, d = h.shape
    ts = min(ts, s)
    halo_blocks = ts // MAX_WINDOW
    return pl.pallas_call(
        functools.partial(_pool_kernel, ts=ts),
        grid=(b, s // ts),
        in_specs=[
            pl.BlockSpec((1, ts, d), lambda bi, i: (bi, i, 0)),
            pl.BlockSpec((1, MAX_WINDOW, d),
                         lambda bi, i: (bi, jnp.maximum(i * halo_blocks - 1, 0), 0)),
            _const_spec((1, d)),
            _const_spec(w_groups.shape),
            _const_spec((1, d)),
        ],
        out_specs=pl.BlockSpec((1, ts, d), lambda bi, i: (bi, i, 0)),
        out_shape=jax.ShapeDtypeStruct((b, s, d), F32),
        scratch_shapes=[pltpu.VMEM((ts + MAX_WINDOW, d), F32)],
        compiler_params=_params(("arbitrary", "arbitrary")),
        name="pool",
    )(h, h, g.reshape(1, d), w_groups, scale.reshape(1, d))


def _kv_kernel(h_ref, g_ref, w_ref, cg_ref, cos_ref, sin_ref, k_ref, vt_ref):
    u = _rmsnorm(h_ref[0], g_ref[...]).astype(BF16)
    kv = jnp.dot(u, w_ref[...], preferred_element_type=F32)
    c = _rmsnorm(kv[:, 0:LANES], cg_ref[...])
    rope = kv[:, LANES:2 * LANES] * cos_ref[...] + kv[:, 2 * LANES:] * sin_ref[...]
    k_ref[0, 0, :, 0:LANES] = c.astype(BF16)
    k_ref[0, 0, :, LANES:] = rope.astype(BF16)
    vt_ref[0, 0] = c.T.astype(BF16)


def _kv(h, g, w_kv3, ckv_g, cos_k, sin_k, *, tk):
    b, s, d = h.shape
    nkt = s // tk
    return pl.pallas_call(
        _kv_kernel,
        grid=(b, nkt),
        in_specs=[
            pl.BlockSpec((1, tk, d), lambda bi, i: (bi, i, 0)),
            _const_spec((1, d)),
            _const_spec(w_kv3.shape),
            _const_spec((1, KV_LORA_RANK)),
            pl.BlockSpec((tk, LANES), lambda bi, i: (i, 0)),
            pl.BlockSpec((tk, LANES), lambda bi, i: (i, 0)),
        ],
        out_specs=[
            pl.BlockSpec((1, 1, tk, QK_PAD), lambda bi, i: (bi, i, 0, 0)),
            pl.BlockSpec((1, 1, KV_LORA_RANK, tk), lambda bi, i: (bi, i, 0, 0)),
        ],
        out_shape=[
            jax.ShapeDtypeStruct((b, nkt, tk, QK_PAD), BF16),
            jax.ShapeDtypeStruct((b, nkt, KV_LORA_RANK, tk), BF16),
        ],
        compiler_params=_params(("arbitrary", "arbitrary")),
        name="kv",
    )(h, g.reshape(1, d), w_kv3, ckv_g.reshape(1, KV_LORA_RANK), cos_k, sin_k)


def _absorb_q_kernel(uk_ref, uq_ref, out_ref):
    out_ref[0] = lax.dot_general(uk_ref[0], uq_ref[0], (((1,), (1,)), ((), ())),
                                 precision=lax.Precision.HIGHEST,
                                 preferred_element_type=F32)


def _absorb_o_kernel(uv_ref, wo_ref, out_ref):
    out_ref[0] = jnp.dot(uv_ref[0], wo_ref[0], precision=lax.Precision.HIGHEST,
                         preferred_element_type=F32).astype(out_ref.dtype)


def _absorb_q(uk_h, uq_h):
    nh, lat, _ = uk_h.shape
    rank = uq_h.shape[1]
    return pl.pallas_call(
        _absorb_q_kernel,
        grid=(nh,),
        in_specs=[pl.BlockSpec((1,) + uk_h.shape[1:], lambda h: (h, 0, 0)),
                  pl.BlockSpec((1,) + uq_h.shape[1:], lambda h: (h, 0, 0))],
        out_specs=pl.BlockSpec((1, lat, rank), lambda h: (h, 0, 0)),
        out_shape=jax.ShapeDtypeStruct((nh, lat, rank), F32),
        compiler_params=_params(("arbitrary",)),
        name="absorb_q",
    )(uk_h, uq_h)


def _absorb_o(uv_h, wo_h):
    nh, lat, _ = uv_h.shape
    n = wo_h.shape[2]
    return pl.pallas_call(
        _absorb_o_kernel,
        grid=(nh,),
        in_specs=[pl.BlockSpec((1,) + uv_h.shape[1:], lambda h: (h, 0, 0)),
                  pl.BlockSpec((1,) + wo_h.shape[1:], lambda h: (h, 0, 0))],
        out_specs=pl.BlockSpec((1, lat, n), lambda h: (h, 0, 0)),
        out_shape=jax.ShapeDtypeStruct((nh, lat, n), BF16),
        compiler_params=_params(("arbitrary",)),
        name="absorb_o",
    )(uv_h, wo_h)


def _qproj_kernel(h_ref, g_ref, wdq_ref, qg_ref, wq_ref, cos_ref, sin_ref, qt_ref):
    u = _rmsnorm(h_ref[0], g_ref[...]).astype(BF16)
    cq = _rmsnorm(jnp.dot(u, wdq_ref[...], preferred_element_type=F32),
                  qg_ref[...]).astype(BF16)
    cos = cos_ref[...]
    sin = sin_ref[...]
    scale = QK_DIM ** -0.5
    tm = cq.shape[0]
    r0 = KV_LORA_RANK
    for hd in range(wq_ref.shape[0]):
        qt = lax.dot_general(wq_ref[hd], cq, (((1,), (1,)), ((), ())),
                             preferred_element_type=F32) * scale
        x1 = qt[r0:r0 + HALF_ROPE]
        x2 = qt[r0 + HALF_ROPE:r0 + QK_ROPE_DIM]
        qt_ref[0, hd, 0:r0, :] = qt[0:r0].astype(BF16)
        qt_ref[0, hd, r0:r0 + HALF_ROPE, :] = (x1 * cos - x2 * sin).astype(BF16)
        qt_ref[0, hd, r0 + HALF_ROPE:r0 + QK_ROPE_DIM, :] = (x1 * sin + x2 * cos).astype(BF16)
        qt_ref[0, hd, r0 + QK_ROPE_DIM:, :] = jnp.zeros((QK_PAD - r0 - QK_ROPE_DIM, tm), BF16)


def _qproj(h, g, wdq, qg, wq_t, cos_t, sin_t, *, tm=256):
    b, s, d = h.shape
    nh = wq_t.shape[0]
    rank = wdq.shape[1]
    tm = min(tm, s)
    return pl.pallas_call(
        _qproj_kernel,
        grid=(b, s // tm),
        in_specs=[
            pl.BlockSpec((1, tm, d), lambda bi, i: (bi, i, 0)),
            _const_spec((1, d)),
            _const_spec(wdq.shape),
            _const_spec((1, rank)),
            _const_spec(wq_t.shape),
            pl.BlockSpec((HALF_ROPE, tm), lambda bi, i: (0, i)),
            pl.BlockSpec((HALF_ROPE, tm), lambda bi, i: (0, i)),
        ],
        out_specs=pl.BlockSpec((1, nh, QK_PAD, tm), lambda bi, i: (bi, 0, 0, i)),
        out_shape=jax.ShapeDtypeStruct((b, nh, QK_PAD, s), BF16),
        compiler_params=_params(("arbitrary", "arbitrary")),
        name="qproj",
    )(h, g.reshape(1, d), wdq, qg.reshape(1, rank), wq_t, cos_t, sin_t)


def _attn_kernel(qt_ref, k_ref, vt_ref, o_ref, acc_ref, m_ref, l_ref, *, hb, tq, tk):
    qi = pl.program_id(2)
    m_ref[...] = jnp.full(m_ref.shape, MASK_VALUE, F32)
    l_ref[...] = jnp.zeros(l_ref.shape, F32)
    acc_ref[...] = jnp.zeros(acc_ref.shape, F32)

    def step(kt, masked):
        k = k_ref[0, kt]
        vt = vt_ref[0, kt]
        if masked:
            kpos = kt * tk + lax.broadcasted_iota(jnp.int32, (tk, tq), 0)
            qpos = qi * tq + lax.broadcasted_iota(jnp.int32, (tk, tq), 1)
            keep = kpos <= qpos
        for j in range(hb):
            s = jnp.dot(k, qt_ref[0, j], preferred_element_type=F32)
            if masked:
                s = jnp.where(keep, s, MASK_VALUE)
            m_old = m_ref[j]
            m_new = jnp.maximum(m_old, jnp.max(s, axis=0, keepdims=True))
            alpha = jnp.exp(m_old - m_new)
            p = jnp.exp(s - m_new)
            l_ref[j] = alpha * l_ref[j] + jnp.sum(p, axis=0, keepdims=True)
            m_ref[j] = m_new
            pv = jnp.dot(vt, p.astype(BF16), preferred_element_type=F32)
            acc_ref[j] = alpha * acc_ref[j] + pv

    def body(kt, carry):
        step(kt, False)
        return carry

    lax.fori_loop(0, qi, body, 0)
    step(qi, True)

    for j in range(hb):
        o = acc_ref[j] / l_ref[j]
        o_ref[0, :, j * V_DIM:(j + 1) * V_DIM] = o.T.astype(o_ref.dtype)


def _attn(qt, k, vt, *, hb=4, tq=256):
    b, nh, _, s = qt.shape
    nkt, tk = k.shape[1], k.shape[2]
    assert tq == tk
    return pl.pallas_call(
        functools.partial(_attn_kernel, hb=hb, tq=tq, tk=tk),
        grid=(b, nh // hb, s // tq),
        in_specs=[
            pl.BlockSpec((1, hb, QK_PAD, tq), lambda bi, hg, qi: (bi, hg, 0, qi)),
            pl.BlockSpec((1, nkt, tk, QK_PAD), lambda bi, hg, qi: (bi, 0, 0, 0),
                         pipeline_mode=pl.Buffered(1)),
            pl.BlockSpec((1, nkt, KV_LORA_RANK, tk), lambda bi, hg, qi: (bi, 0, 0, 0),
                         pipeline_mode=pl.Buffered(1)),
        ],
        out_specs=pl.BlockSpec((1, tq, hb * V_DIM), lambda bi, hg, qi: (bi, qi, hg)),
        out_shape=jax.ShapeDtypeStruct((b, s, nh * V_DIM), BF16),
        scratch_shapes=[
            pltpu.VMEM((hb, KV_LORA_RANK, tq), F32),
            pltpu.VMEM((hb, 1, tq), F32),
            pltpu.VMEM((hb, 1, tq), F32),
        ],
        compiler_params=_params(("arbitrary", "arbitrary", "arbitrary")),
        name="attn",
    )(qt, k, vt)


def _rope_tables(seq):
    pos = jnp.arange(seq, dtype=F32)
    inv_freq = ROPE_THETA ** (-jnp.arange(0, QK_ROPE_DIM, 2, dtype=F32) / QK_ROPE_DIM)
    ang = pos[:, None] * inv_freq[None, :]
    return jnp.cos(ang), jnp.sin(ang)


def kernel(x, ffn_pre_norm, ffn_pre_wg, ffn_pre_wu, ffn_pre_wd, mix_norm, ffn_post_norm, ffn_post_wg, ffn_post_wu, ffn_post_wd, pool_w, pool_scale, kv_in_norm, w_dkv, ckv_norm, w_uk, w_uv, q_lora_norm, w_dq, w_uq, w_o, final_norm):
    b, s, d = x.shape
    depth = ffn_pre_norm.shape[0]
    n_a = pool_w.shape[0]
    t = b * s
    tile = min(256, s)

    pre = [w.astype(BF16) for w in (ffn_pre_wg, ffn_pre_wu, ffn_pre_wd)]
    post = [w.astype(BF16) for w in (ffn_post_wg, ffn_post_wu, ffn_post_wd)]
    pool_w_bf = pool_w.astype(BF16)

    cos, sin = _rope_tables(s)
    pad = jnp.zeros((s, LANES - QK_ROPE_DIM), F32)
    cos_k = jnp.concatenate([cos, cos, pad], axis=1)
    sin_k = jnp.concatenate([sin, sin, pad], axis=1)
    cos_t, sin_t = cos.T, sin.T

    wr = w_dkv[:, KV_LORA_RANK:]
    wpad = jnp.zeros((d, LANES - QK_ROPE_DIM), F32)
    w_kv3 = jnp.concatenate(
        [w_dkv[:, :KV_LORA_RANK], wr, wpad,
         -wr[:, HALF_ROPE:], wr[:, :HALF_ROPE], wpad], axis=1).astype(BF16)

    uk_h = jnp.transpose(w_uk, (1, 0, 2))
    uv_h = jnp.transpose(w_uv, (1, 0, 2))
    n_b = w_dq.shape[0]
    wq_ts, wo_abs = [], []
    for j in range(n_b):
        uq_h = jnp.transpose(w_uq[j], (1, 0, 2))
        a_nope = _absorb_q(uk_h, uq_h[:, :, :QK_NOPE_DIM])
        a_rope = jnp.transpose(uq_h[:, :, QK_NOPE_DIM:], (0, 2, 1))
        wq_ts.append(jnp.concatenate([a_nope, a_rope], axis=1).astype(BF16))
        wo_h = w_o[j].reshape(N_HEADS, V_DIM, d)
        wo_abs.append(_absorb_o(uv_h, wo_h).reshape(N_HEADS * KV_LORA_RANK, d))

    h = x.reshape(t, d)
    k_lat = vt_lat = None
    o_prev = wo_prev = None
    for l in range(depth):
        h = _ffn(h, ffn_pre_norm[l], pre[0][l], pre[1][l], pre[2][l])
        if l < n_a:
            h = _pool(h.reshape(b, s, d), mix_norm[l], pool_w_bf[l], pool_scale[l]).reshape(t, d)
            o_prev = wo_prev = None
        else:
            j = l - n_a
            qt = _qproj(h.reshape(b, s, d), mix_norm[l], w_dq[j].astype(BF16),
                        q_lora_norm[j], wq_ts[j], cos_t, sin_t, tm=tile)
            o_prev = _attn(qt, k_lat, vt_lat, tq=tile).reshape(t, N_HEADS * V_DIM)
            wo_prev = wo_abs[j]
        h = _ffn(h, ffn_post_norm[l], post[0][l], post[1][l], post[2][l],
                 o2d=o_prev, wo=wo_prev,
                 final_g=final_norm if l == depth - 1 else None)
        if l == n_a - 1:
            k_lat, vt_lat = _kv(h.reshape(b, s, d), kv_in_norm, w_kv3, ckv_norm,
                                cos_k, sin_k, tk=tile)
    return h.reshape(b, s, d)
```

```python
import functools

import jax
import jax.numpy as jnp
from jax import lax
from jax.experimental import pallas as pl
from jax.experimental.pallas import tpu as pltpu

RMS_EPS = 1e-6
POOL_WINDOWS = (2, 4, 8, 16)
MAX_WINDOW = max(POOL_WINDOWS)
N_HEADS = 16
QK_NOPE_DIM = 128
QK_ROPE_DIM = 64
QK_DIM = QK_NOPE_DIM + QK_ROPE_DIM
V_DIM = 128
KV_LORA_RANK = 128
ROPE_THETA = 10000.0
HALF_ROPE = QK_ROPE_DIM // 2

VMEM_LIMIT_BYTES = 56 * 1024 * 1024
LANES = 128
QK_PAD = 256
BF16_SUBLANES = 16
VT_ROWS = KV_LORA_RANK + BF16_SUBLANES
LOG2E = 1.4426950408889634
MASK_VALUE = -1e30

BF16 = jnp.bfloat16
F32 = jnp.float32


def _rmsnorm(x, g):
    ms = jnp.mean(x * x, axis=-1, keepdims=True)
    return x * lax.rsqrt(ms + RMS_EPS) * g


def _const_spec(shape):
    nd = len(shape)
    return pl.BlockSpec(shape, lambda *_: (0,) * nd, pipeline_mode=pl.Buffered(1))


def _params(sem):
    return pltpu.CompilerParams(dimension_semantics=sem,
                                vmem_limit_bytes=VMEM_LIMIT_BYTES)


def _ffn_kernel(*refs, tf, with_oproj, with_final):
    it = iter(refs)
    h_ref = next(it)
    if with_oproj:
        o_ref = next(it)
        wo_ref = next(it)
    g_ref, wg_ref, wu_ref, wd_ref = next(it), next(it), next(it), next(it)
    if with_final:
        fg_ref = next(it)
    out_ref = next(it)
    a_ref = next(it)

    x = h_ref[...]
    if with_oproj:
        x = x + jnp.dot(o_ref[...], wo_ref[...], preferred_element_type=F32)
    u = _rmsnorm(x, g_ref[...]).astype(BF16)
    d_ff = wg_ref.shape[1]
    for c in range(d_ff // tf):
        sl = slice(c * tf, (c + 1) * tf)
        gate = jnp.dot(u, wg_ref[:, sl], preferred_element_type=F32)
        up = jnp.dot(u, wu_ref[:, sl], preferred_element_type=F32)
        a_ref[:, sl] = (gate * jax.nn.sigmoid(gate) * up).astype(BF16)
    y = jnp.dot(a_ref[...], wd_ref[...], preferred_element_type=F32)
    res = x + 0.5 * y
    if with_final:
        res = _rmsnorm(res, fg_ref[...])
    out_ref[...] = res


def _ffn(h2d, g, wg, wu, wd, *, o2d=None, wo=None, final_g=None, tm=512, tf=256):
    t, d = h2d.shape
    d_ff = wg.shape[1]
    tm = min(tm, t)
    row = lambda i: (i, 0)
    args = [h2d]
    specs = [pl.BlockSpec((tm, d), row)]
    if o2d is not None:
        args += [o2d, wo]
        specs += [pl.BlockSpec((tm, o2d.shape[1]), row), _const_spec(wo.shape)]
    args += [g.reshape(1, d), wg, wu, wd]
    specs += [_const_spec((1, d)), _const_spec(wg.shape), _const_spec(wu.shape),
              _const_spec(wd.shape)]
    if final_g is not None:
        args.append(final_g.reshape(1, d))
        specs.append(_const_spec((1, d)))
    kern = functools.partial(_ffn_kernel, tf=tf, with_oproj=o2d is not None,
                             with_final=final_g is not None)
    return pl.pallas_call(
        kern,
        grid=(t // tm,),
        in_specs=specs,
        out_specs=pl.BlockSpec((tm, d), row),
        out_shape=jax.ShapeDtypeStruct((t, d), F32),
        scratch_shapes=[pltpu.VMEM((tm, d_ff), BF16)],
        compiler_params=_params(("arbitrary",)),
        name="ffn",
    )(*args)


def _pool_kernel(h_ref, halo_ref, g_ref, w_ref, sc_ref, out_ref, ext_ref, *, ts):
    i = pl.program_id(1)
    g = g_ref[...]
    x = h_ref[0]
    u = _rmsnorm(x, g)
    halo = _rmsnorm(halo_ref[0], g)
    ext_ref[0:MAX_WINDOW, :] = jnp.where(i > 0, halo, 0.0)
    ext_ref[MAX_WINDOW:, :] = u
    pos = i * ts + lax.broadcasted_iota(jnp.int32, (ts, 1), 0)
    dg = w_ref.shape[1]
    for gi, w in enumerate(POOL_WINDOWS):
        cs = slice(gi * dg, (gi + 1) * dg)
        ug = ext_ref[MAX_WINDOW:MAX_WINDOW + ts, cs]
        win = ug
        for j in range(1, w):
            win = win + ext_ref[MAX_WINDOW - j:MAX_WINDOW - j + ts, cs]
        count = jnp.minimum(pos + 1, w).astype(F32)
        y = (win / count - ug).astype(BF16)
        z = jnp.dot(y, w_ref[gi], preferred_element_type=F32)
        out_ref[0, :, cs] = x[:, cs] + z * sc_ref[:, cs]


def _pool(h, g, w_groups, scale, *, ts=512):
    b, s, d = h.shape
    ts = min(ts, s)
    halo_blocks = ts // MAX_WINDOW
    return pl.pallas_call(
        functools.partial(_pool_kernel, ts=ts),
        grid=(b, s // ts),
        in_specs=[
            pl.BlockSpec((1, ts, d), lambda bi, i: (bi, i, 0)),
            pl.BlockSpec((1, MAX_WINDOW, d),
                         lambda bi, i: (bi, jnp.maximum(i * halo_blocks - 1, 0), 0)),
            _const_spec((1, d)),
            _const_spec(w_groups.shape),
            _const_spec((1, d)),
        ],
        out_specs=pl.BlockSpec((1, ts, d), lambda bi, i: (bi, i, 0)),
        out_shape=jax.ShapeDtypeStruct((b, s, d), F32),
        scratch_shapes=[pltpu.VMEM((ts + MAX_WINDOW, d), F32)],
        compiler_params=_params(("arbitrary", "arbitrary")),
        name="pool",
    )(h, h, g.reshape(1, d), w_groups, scale.reshape(1, d))


def _kv_kernel(h_ref, g_ref, w_ref, cg_ref, cos_ref, sin_ref, k_ref, vt_ref):
    u = _rmsnorm(h_ref[0], g_ref[...]).astype(BF16)
    kv = jnp.dot(u, w_ref[...], preferred_element_type=F32)
    c = _rmsnorm(kv[:, 0:LANES], cg_ref[...])
    rope = kv[:, LANES:2 * LANES] * cos_ref[...] + kv[:, 2 * LANES:] * sin_ref[...]
    k_ref[0, 0, :, 0:LANES] = c.astype(BF16)
    k_ref[0, 0, :, LANES:] = rope.astype(BF16)
    vt_ref[0, 0, 0:KV_LORA_RANK, :] = c.T.astype(BF16)
    extra = (BF16_SUBLANES, c.shape[0])
    ones_row = lax.broadcasted_iota(jnp.int32, extra, 0) == 0
    vt_ref[0, 0, KV_LORA_RANK:, :] = jnp.where(ones_row, 1.0, 0.0).astype(BF16)


def _kv(h, g, w_kv3, ckv_g, cos_k, sin_k, *, tk):
    b, s, d = h.shape
    nkt = s // tk
    return pl.pallas_call(
        _kv_kernel,
        grid=(b, nkt),
        in_specs=[
            pl.BlockSpec((1, tk, d), lambda bi, i: (bi, i, 0)),
            _const_spec((1, d)),
            _const_spec(w_kv3.shape),
            _const_spec((1, KV_LORA_RANK)),
            pl.BlockSpec((tk, LANES), lambda bi, i: (i, 0)),
            pl.BlockSpec((tk, LANES), lambda bi, i: (i, 0)),
        ],
        out_specs=[
            pl.BlockSpec((1, 1, tk, QK_PAD), lambda bi, i: (bi, i, 0, 0)),
            pl.BlockSpec((1, 1, VT_ROWS, tk), lambda bi, i: (bi, i, 0, 0)),
        ],
        out_shape=[
            jax.ShapeDtypeStruct((b, nkt, tk, QK_PAD), BF16),
            jax.ShapeDtypeStruct((b, nkt, VT_ROWS, tk), BF16),
        ],
        compiler_params=_params(("arbitrary", "arbitrary")),
        name="kv",
    )(h, g.reshape(1, d), w_kv3, ckv_g.reshape(1, KV_LORA_RANK), cos_k, sin_k)


def _absorb_q_kernel(uk_ref, uq_ref, out_ref):
    out_ref[0] = lax.dot_general(uk_ref[0], uq_ref[0], (((1,), (1,)), ((), ())),
                                 precision=lax.Precision.HIGHEST,
                                 preferred_element_type=F32)


def _absorb_o_kernel(uv_ref, wo_ref, out_ref):
    out_ref[0] = jnp.dot(uv_ref[0], wo_ref[0], precision=lax.Precision.HIGHEST,
                         preferred_element_type=F32).astype(out_ref.dtype)


def _absorb_q(uk_h, uq_h):
    nh, lat, _ = uk_h.shape
    rank = uq_h.shape[1]
    return pl.pallas_call(
        _absorb_q_kernel,
        grid=(nh,),
        in_specs=[pl.BlockSpec((1,) + uk_h.shape[1:], lambda h: (h, 0, 0)),
                  pl.BlockSpec((1,) + uq_h.shape[1:], lambda h: (h, 0, 0))],
        out_specs=pl.BlockSpec((1, lat, rank), lambda h: (h, 0, 0)),
        out_shape=jax.ShapeDtypeStruct((nh, lat, rank), F32),
        compiler_params=_params(("arbitrary",)),
        name="absorb_q",
    )(uk_h, uq_h)


def _absorb_o(uv_h, wo_h):
    nh, lat, _ = uv_h.shape
    n = wo_h.shape[2]
    return pl.pallas_call(
        _absorb_o_kernel,
        grid=(nh,),
        in_specs=[pl.BlockSpec((1,) + uv_h.shape[1:], lambda h: (h, 0, 0)),
                  pl.BlockSpec((1,) + wo_h.shape[1:], lambda h: (h, 0, 0))],
        out_specs=pl.BlockSpec((1, lat, n), lambda h: (h, 0, 0)),
        out_shape=jax.ShapeDtypeStruct((nh, lat, n), BF16),
        compiler_params=_params(("arbitrary",)),
        name="absorb_o",
    )(uv_h, wo_h)


def _qproj_kernel(h_ref, g_ref, wdq_ref, qg_ref, wq_ref, cos_ref, sin_ref, qt_ref):
    u = _rmsnorm(h_ref[0], g_ref[...]).astype(BF16)
    cq = _rmsnorm(jnp.dot(u, wdq_ref[...], preferred_element_type=F32),
                  qg_ref[...]).astype(BF16)
    cos = cos_ref[...]
    sin = sin_ref[...]
    scale = QK_DIM ** -0.5 * LOG2E
    tm = cq.shape[0]
    r0 = KV_LORA_RANK
    for hd in range(wq_ref.shape[0]):
        qt = lax.dot_general(wq_ref[hd], cq, (((1,), (1,)), ((), ())),
                             preferred_element_type=F32) * scale
        x1 = qt[r0:r0 + HALF_ROPE]
        x2 = qt[r0 + HALF_ROPE:r0 + QK_ROPE_DIM]
        qt_ref[0, hd, 0:r0, :] = qt[0:r0].astype(BF16)
        qt_ref[0, hd, r0:r0 + HALF_ROPE, :] = (x1 * cos - x2 * sin).astype(BF16)
        qt_ref[0, hd, r0 + HALF_ROPE:r0 + QK_ROPE_DIM, :] = (x1 * sin + x2 * cos).astype(BF16)
        qt_ref[0, hd, r0 + QK_ROPE_DIM:, :] = jnp.zeros((QK_PAD - r0 - QK_ROPE_DIM, tm), BF16)


def _qproj(h, g, wdq, qg, wq_t, cos_t, sin_t, *, tm=256):
    b, s, d = h.shape
    nh = wq_t.shape[0]
    rank = wdq.shape[1]
    tm = min(tm, s)
    return pl.pallas_call(
        _qproj_kernel,
        grid=(b, s // tm),
        in_specs=[
            pl.BlockSpec((1, tm, d), lambda bi, i: (bi, i, 0)),
            _const_spec((1, d)),
            _const_spec(wdq.shape),
            _const_spec((1, rank)),
            _const_spec(wq_t.shape),
            pl.BlockSpec((HALF_ROPE, tm), lambda bi, i: (0, i)),
            pl.BlockSpec((HALF_ROPE, tm), lambda bi, i: (0, i)),
        ],
        out_specs=pl.BlockSpec((1, nh, QK_PAD, tm), lambda bi, i: (bi, 0, 0, i)),
        out_shape=jax.ShapeDtypeStruct((b, nh, QK_PAD, s), BF16),
        compiler_params=_params(("arbitrary", "arbitrary")),
        name="qproj",
    )(h, g.reshape(1, d), wdq, qg.reshape(1, rank), wq_t, cos_t, sin_t)


def _attn_kernel(qt_ref, k_ref, vt_ref, o_ref, acc_ref, m_ref, s_ref, p_ref, cm_ref, *,
                 hb, tq, tk):
    qi = pl.program_id(2)
    kd = (qi * tq) // tk
    m_ref[...] = jnp.full(m_ref.shape, MASK_VALUE, F32)
    acc_ref[...] = jnp.zeros(acc_ref.shape, F32)

    def qk_stage(kt, slot):
        k = k_ref[0, kt]
        for j in range(hb):
            s = jnp.dot(k, qt_ref[0, j], preferred_element_type=F32)
            s_ref[slot, j] = s
            cm_ref[slot, j] = jnp.max(s, axis=0, keepdims=True)

    def softmax_pv_stage(kt, slot, masked):
        vt = vt_ref[0, kt]
        if masked:
            kpos = kt * tk + lax.broadcasted_iota(jnp.int32, (tk, tq), 0)
            qpos = qi * tq + lax.broadcasted_iota(jnp.int32, (tk, tq), 1)
            keep = kpos <= qpos
        alphas = []
        for j in range(hb):
            s = s_ref[slot, j]
            if masked:
                s = jnp.where(keep, s, MASK_VALUE)
                cm = jnp.max(s, axis=0, keepdims=True)
            else:
                cm = cm_ref[slot, j]
            m_old = m_ref[j]
            m_new = jnp.maximum(m_old, cm)
            m_ref[j] = m_new
            alphas.append(jnp.exp2(m_old - m_new))
            p_ref[j] = jnp.exp2(s - m_new).astype(BF16)
        for j in range(hb):
            pv = jnp.dot(vt, p_ref[j], preferred_element_type=F32)
            acc_ref[j] = alphas[j] * acc_ref[j] + pv

    qk_stage(kd, 0)
    qk_stage(jnp.maximum(kd - 1, 0), 1)
    softmax_pv_stage(kd, 0, True)

    def body(i, carry):
        idx = 2 * i + 1
        qk_stage(kd - (idx + 1), 0)
        softmax_pv_stage(kd - idx, 1, False)
        qk_stage(jnp.maximum(kd - (idx + 2), 0), 1)
        softmax_pv_stage(kd - (idx + 1), 0, False)
        return carry

    lax.fori_loop(0, kd // 2, body, 0)

    @pl.when(lax.rem(kd, 2) == 1)
    def _():
        softmax_pv_stage(0, 1, False)

    for j in range(hb):
        o = acc_ref[j, 0:V_DIM, :] / acc_ref[j, V_DIM:V_DIM + 1, :]
        o_ref[0, :, j * V_DIM:(j + 1) * V_DIM] = o.T.astype(o_ref.dtype)


def _attn(qt, k, vt, *, hb=8, tq=256):
    b, nh, _, s = qt.shape
    nkt, tk = k.shape[1], k.shape[2]
    assert tk % tq == 0 and vt.shape[2] == VT_ROWS
    return pl.pallas_call(
        functools.partial(_attn_kernel, hb=hb, tq=tq, tk=tk),
        grid=(b, nh // hb, s // tq),
        in_specs=[
            pl.BlockSpec((1, hb, QK_PAD, tq), lambda bi, hg, qi: (bi, hg, 0, qi)),
            pl.BlockSpec((1, nkt, tk, QK_PAD), lambda bi, hg, qi: (bi, 0, 0, 0),
                         pipeline_mode=pl.Buffered(1)),
            pl.BlockSpec((1, nkt, VT_ROWS, tk), lambda bi, hg, qi: (bi, 0, 0, 0),
                         pipeline_mode=pl.Buffered(1)),
        ],
        out_specs=pl.BlockSpec((1, tq, hb * V_DIM), lambda bi, hg, qi: (bi, qi, hg)),
        out_shape=jax.ShapeDtypeStruct((b, s, nh * V_DIM), BF16),
        scratch_shapes=[
            pltpu.VMEM((hb, VT_ROWS, tq), F32),
            pltpu.VMEM((hb, 1, tq), F32),
            pltpu.VMEM((2, hb, tk, tq), F32),
            pltpu.VMEM((hb, tk, tq), BF16),
            pltpu.VMEM((2, hb, 1, tq), F32),
        ],
        compiler_params=_params(("arbitrary", "arbitrary", "arbitrary")),
        name="attn",
    )(qt, k, vt)


def _rope_tables(seq):
    pos = jnp.arange(seq, dtype=F32)
    inv_freq = ROPE_THETA ** (-jnp.arange(0, QK_ROPE_DIM, 2, dtype=F32) / QK_ROPE_DIM)
    ang = pos[:, None] * inv_freq[None, :]
    return jnp.cos(ang), jnp.sin(ang)


def kernel(x, ffn_pre_norm, ffn_pre_wg, ffn_pre_wu, ffn_pre_wd, mix_norm, ffn_post_norm, ffn_post_wg, ffn_post_wu, ffn_post_wd, pool_w, pool_scale, kv_in_norm, w_dkv, ckv_norm, w_uk, w_uv, q_lora_norm, w_dq, w_uq, w_o, final_norm):
    b, s, d = x.shape
    depth = ffn_pre_norm.shape[0]
    n_a = pool_w.shape[0]
    t = b * s
    tq = min(256, s)
    tk = min(512, s)

    pre = [w.astype(BF16) for w in (ffn_pre_wg, ffn_pre_wu, ffn_pre_wd)]
    post = [w.astype(BF16) for w in (ffn_post_wg, ffn_post_wu, ffn_post_wd)]
    pool_w_bf = pool_w.astype(BF16)

    cos, sin = _rope_tables(s)
    pad = jnp.zeros((s, LANES - QK_ROPE_DIM), F32)
    cos_k = jnp.concatenate([cos, cos, pad], axis=1)
    sin_k = jnp.concatenate([sin, sin, pad], axis=1)
    cos_t, sin_t = cos.T, sin.T

    wr = w_dkv[:, KV_LORA_RANK:]
    wpad = jnp.zeros((d, LANES - QK_ROPE_DIM), F32)
    w_kv3 = jnp.concatenate(
        [w_dkv[:, :KV_LORA_RANK], wr, wpad,
         -wr[:, HALF_ROPE:], wr[:, :HALF_ROPE], wpad], axis=1).astype(BF16)

    uk_h = jnp.transpose(w_uk, (1, 0, 2))
    uv_h = jnp.transpose(w_uv, (1, 0, 2))
    n_b = w_dq.shape[0]
    wq_ts, wo_abs = [], []
    for j in range(n_b):
        uq_h = jnp.transpose(w_uq[j], (1, 0, 2))
        a_nope = _absorb_q(uk_h, uq_h[:, :, :QK_NOPE_DIM])
        a_rope = jnp.transpose(uq_h[:, :, QK_NOPE_DIM:], (0, 2, 1))
        wq_ts.append(jnp.concatenate([a_nope, a_rope], axis=1).astype(BF16))
        wo_h = w_o[j].reshape(N_HEADS, V_DIM, d)
        wo_abs.append(_absorb_o(uv_h, wo_h).reshape(N_HEADS * KV_LORA_RANK, d))

    h = x.reshape(t, d)
    k_lat = vt_lat = None
    o_prev = wo_prev = None
    for l in range(depth):
        h = _ffn(h, ffn_pre_norm[l], pre[0][l], pre[1][l], pre[2][l])
        if l < n_a:
            h = _pool(h.reshape(b, s, d), mix_norm[l], pool_w_bf[l], pool_scale[l]).reshape(t, d)
            o_prev = wo_prev = None
        else:
            j = l - n_a
            qt = _qproj(h.reshape(b, s, d), mix_norm[l], w_dq[j].astype(BF16),
                        q_lora_norm[j], wq_ts[j], cos_t, sin_t, tm=tq)
            o_prev = _attn(qt, k_lat, vt_lat, tq=tq).reshape(t, N_HEADS * V_DIM)
            wo_prev = wo_abs[j]
        h = _ffn(h, ffn_post_norm[l], post[0][l], post[1][l], post[2][l],
                 o2d=o_prev, wo=wo_prev,
                 final_g=final_norm if l == depth - 1 else None)
        if l == n_a - 1:
            k_lat, vt_lat = _kv(h.reshape(b, s, d), kv_in_norm, w_kv3, ckv_norm,
                                cos_k, sin_k, tk=tk)
    return h.reshape(b, s, d)
```

```python
import functools

import jax
import jax.numpy as jnp
from jax import lax
from jax.experimental import pallas as pl
from jax.experimental.pallas import tpu as pltpu

RMS_EPS = 1e-6
POOL_WINDOWS = (2, 4, 8, 16)
MAX_WINDOW = max(POOL_WINDOWS)
N_HEADS = 16
QK_NOPE_DIM = 128
QK_ROPE_DIM = 64
QK_DIM = QK_NOPE_DIM + QK_ROPE_DIM
V_DIM = 128
KV_LORA_RANK = 128
ROPE_THETA = 10000.0
HALF_ROPE = QK_ROPE_DIM // 2

VMEM_LIMIT_BYTES = 56 * 1024 * 1024
LANES = 128
QK_PAD = 256
BF16_SUBLANES = 16
VT_ROWS = KV_LORA_RANK + BF16_SUBLANES
LOG2E = 1.4426950408889634
MASK_VALUE = -1e30

BF16 = jnp.bfloat16
F32 = jnp.float32


def _rmsnorm(x, g):
    ms = jnp.mean(x * x, axis=-1, keepdims=True)
    return x * lax.rsqrt(ms + RMS_EPS) * g


def _const_spec(shape):
    nd = len(shape)
    return pl.BlockSpec(shape, lambda *_: (0,) * nd, pipeline_mode=pl.Buffered(1))


def _params(sem):
    return pltpu.CompilerParams(dimension_semantics=sem,
                                vmem_limit_bytes=VMEM_LIMIT_BYTES)


def _ffn_kernel(*refs, tf, with_oproj, with_final):
    it = iter(refs)
    h_ref = next(it)
    if with_oproj:
        o_ref = next(it)
        wo_ref = next(it)
    g_ref, wg_ref, wu_ref, wd_ref = next(it), next(it), next(it), next(it)
    if with_final:
        fg_ref = next(it)
    out_ref = next(it)
    a_ref = next(it)

    x = h_ref[...]
    if with_oproj:
        x = x + jnp.dot(o_ref[...], wo_ref[...], preferred_element_type=F32)
    u = _rmsnorm(x, g_ref[...]).astype(BF16)
    d_ff = wg_ref.shape[1]
    for c in range(d_ff // tf):
        sl = slice(c * tf, (c + 1) * tf)
        gate = jnp.dot(u, wg_ref[:, sl], preferred_element_type=F32)
        up = jnp.dot(u, wu_ref[:, sl], preferred_element_type=F32)
        a_ref[:, sl] = (gate * jax.nn.sigmoid(gate) * up).astype(BF16)
    y = jnp.dot(a_ref[...], wd_ref[...], preferred_element_type=F32)
    res = x + 0.5 * y
    if with_final:
        res = _rmsnorm(res, fg_ref[...])
    out_ref[...] = res


def _ffn(h2d, g, wg, wu, wd, *, o2d=None, wo=None, final_g=None, tm=512, tf=256):
    t, d = h2d.shape
    d_ff = wg.shape[1]
    tm = min(tm, t)
    row = lambda i: (i, 0)
    args = [h2d]
    specs = [pl.BlockSpec((tm, d), row)]
    if o2d is not None:
        args += [o2d, wo]
        specs += [pl.BlockSpec((tm, o2d.shape[1]), row), _const_spec(wo.shape)]
    args += [g.reshape(1, d), wg, wu, wd]
    specs += [_const_spec((1, d)), _const_spec(wg.shape), _const_spec(wu.shape),
              _const_spec(wd.shape)]
    if final_g is not None:
        args.append(final_g.reshape(1, d))
        specs.append(_const_spec((1, d)))
    kern = functools.partial(_ffn_kernel, tf=tf, with_oproj=o2d is not None,
                             with_final=final_g is not None)
    return pl.pallas_call(
        kern,
        grid=(t // tm,),
        in_specs=specs,
        out_specs=pl.BlockSpec((tm, d), row),
        out_shape=jax.ShapeDtypeStruct((t, d), F32),
        scratch_shapes=[pltpu.VMEM((tm, d_ff), BF16)],
        compiler_params=_params(("arbitrary",)),
        name="ffn",
    )(*args)


def _pool_kernel(h_ref, halo_ref, g_ref, w_ref, sc_ref, out_ref, ext_ref, *, ts):
    i = pl.program_id(1)
    g = g_ref[...]
    x = h_ref[0]
    u = _rmsnorm(x, g)
    halo = _rmsnorm(halo_ref[0], g)
    ext_ref[0:MAX_WINDOW, :] = jnp.where(i > 0, halo, 0.0)
    ext_ref[MAX_WINDOW:, :] = u
    pos = i * ts + lax.broadcasted_iota(jnp.int32, (ts, 1), 0)
    dg = w_ref.shape[1]
    for gi, w in enumerate(POOL_WINDOWS):
        cs = slice(gi * dg, (gi + 1) * dg)
        ug = ext_ref[MAX_WINDOW:MAX_WINDOW + ts, cs]
        win = ug
        for j in range(1, w):
            win = win + ext_ref[MAX_WINDOW - j:MAX_WINDOW - j + ts, cs]
        count = jnp.minimum(pos + 1, w).astype(F32)
        y = (win / count - ug).astype(BF16)
        z = jnp.dot(y, w_ref[gi], preferred_element_type=F32)
        out_ref[0, :, cs] = x[:, cs] + z * sc_ref[:, cs]


def _pool(h, g, w_groups, scale, *, ts=512):
    b, s, d = h.shape
    ts = min(ts, s)
    halo_blocks = ts // MAX_WINDOW
    return pl.pallas_call(
        functools.partial(_pool_kernel, ts=ts),
        grid=(b, s // ts),
        in_specs=[
            pl.BlockSpec((1, ts, d), lambda bi, i: (bi, i, 0)),
            pl.BlockSpec((1, MAX_WINDOW, d),
                         lambda bi, i: (bi, jnp.maximum(i * halo_blocks - 1, 0), 0)),
            _const_spec((1, d)),
            _const_spec(w_groups.shape),
            _const_spec((1, d)),
        ],
        out_specs=pl.BlockSpec((1, ts, d), lambda bi, i: (bi, i, 0)),
        out_shape=jax.ShapeDtypeStruct((b, s, d), F32),
        scratch_shapes=[pltpu.VMEM((ts + MAX_WINDOW, d), F32)],
        compiler_params=_params(("arbitrary", "arbitrary")),
        name="pool",
    )(h, h, g.reshape(1, d), w_groups, scale.reshape(1, d))


def _kv_kernel(h_ref, g_ref, w_ref, cg_ref, cos_ref, sin_ref, k_ref, vt_ref):
    u = _rmsnorm(h_ref[0], g_ref[...]).astype(BF16)
    kv = jnp.dot(u, w_ref[...], preferred_element_type=F32)
    c = _rmsnorm(kv[:, 0:LANES], cg_ref[...])
    rope = kv[:, LANES:2 * LANES] * cos_ref[...] + kv[:, 2 * LANES:] * sin_ref[...]
    k_ref[0, 0, :, 0:LANES] = c.astype(BF16)
    k_ref[0, 0, :, LANES:] = rope.astype(BF16)
    vt_ref[0, 0, 0:KV_LORA_RANK, :] = c.T.astype(BF16)
    extra = (BF16_SUBLANES, c.shape[0])
    ones_row = lax.broadcasted_iota(jnp.int32, extra, 0) == 0
    vt_ref[0, 0, KV_LORA_RANK:, :] = jnp.where(ones_row, 1.0, 0.0).astype(BF16)


def _kv(h, g, w_kv3, ckv_g, cos_k, sin_k, *, tk):
    b, s, d = h.shape
    nkt = s // tk
    return pl.pallas_call(
        _kv_kernel,
        grid=(b, nkt),
        in_specs=[
            pl.BlockSpec((1, tk, d), lambda bi, i: (bi, i, 0)),
            _const_spec((1, d)),
            _const_spec(w_kv3.shape),
            _const_spec((1, KV_LORA_RANK)),
            pl.BlockSpec((tk, LANES), lambda bi, i: (i, 0)),
            pl.BlockSpec((tk, LANES), lambda bi, i: (i, 0)),
        ],
        out_specs=[
            pl.BlockSpec((1, 1, tk, QK_PAD), lambda bi, i: (bi, i, 0, 0)),
            pl.BlockSpec((1, 1, VT_ROWS, tk), lambda bi, i: (bi, i, 0, 0)),
        ],
        out_shape=[
            jax.ShapeDtypeStruct((b, nkt, tk, QK_PAD), BF16),
            jax.ShapeDtypeStruct((b, nkt, VT_ROWS, tk), BF16),
        ],
        compiler_params=_params(("arbitrary", "arbitrary")),
        name="kv",
    )(h, g.reshape(1, d), w_kv3, ckv_g.reshape(1, KV_LORA_RANK), cos_k, sin_k)


def _absorb_q_kernel(uk_ref, uq_ref, out_ref):
    out_ref[0] = lax.dot_general(uk_ref[0], uq_ref[0], (((1,), (1,)), ((), ())),
                                 precision=lax.Precision.HIGHEST,
                                 preferred_element_type=F32)


def _absorb_o_kernel(uv_ref, wo_ref, out_ref):
    out_ref[0] = jnp.dot(uv_ref[0], wo_ref[0], precision=lax.Precision.HIGHEST,
                         preferred_element_type=F32).astype(out_ref.dtype)


def _absorb_q(uk_h, uq_h):
    nh, lat, _ = uk_h.shape
    rank = uq_h.shape[1]
    return pl.pallas_call(
        _absorb_q_kernel,
        grid=(nh,),
        in_specs=[pl.BlockSpec((1,) + uk_h.shape[1:], lambda h: (h, 0, 0)),
                  pl.BlockSpec((1,) + uq_h.shape[1:], lambda h: (h, 0, 0))],
        out_specs=pl.BlockSpec((1, lat, rank), lambda h: (h, 0, 0)),
        out_shape=jax.ShapeDtypeStruct((nh, lat, rank), F32),
        compiler_params=_params(("arbitrary",)),
        name="absorb_q",
    )(uk_h, uq_h)


def _absorb_o(uv_h, wo_h):
    nh, lat, _ = uv_h.shape
    n = wo_h.shape[2]
    return pl.pallas_call(
        _absorb_o_kernel,
        grid=(nh,),
        in_specs=[pl.BlockSpec((1,) + uv_h.shape[1:], lambda h: (h, 0, 0)),
                  pl.BlockSpec((1,) + wo_h.shape[1:], lambda h: (h, 0, 0))],
        out_specs=pl.BlockSpec((1, lat, n), lambda h: (h, 0, 0)),
        out_shape=jax.ShapeDtypeStruct((nh, lat, n), BF16),
        compiler_params=_params(("arbitrary",)),
        name="absorb_o",
    )(uv_h, wo_h)


def _qproj_kernel(h_ref, g_ref, wdq_ref, qg_ref, wq_ref, cos_ref, sin_ref, qt_ref):
    u = _rmsnorm(h_ref[0], g_ref[...]).astype(BF16)
    cq = _rmsnorm(jnp.dot(u, wdq_ref[...], preferred_element_type=F32),
                  qg_ref[...]).astype(BF16)
    cos = cos_ref[...]
    sin = sin_ref[...]
    scale = QK_DIM ** -0.5 * LOG2E
    tm = cq.shape[0]
    r0 = KV_LORA_RANK
    for hd in range(wq_ref.shape[0]):
        qt = lax.dot_general(wq_ref[hd], cq, (((1,), (1,)), ((), ())),
                             preferred_element_type=F32) * scale
        x1 = qt[r0:r0 + HALF_ROPE]
        x2 = qt[r0 + HALF_ROPE:r0 + QK_ROPE_DIM]
        qt_ref[0, hd, 0:r0, :] = qt[0:r0].astype(BF16)
        qt_ref[0, hd, r0:r0 + HALF_ROPE, :] = (x1 * cos - x2 * sin).astype(BF16)
        qt_ref[0, hd, r0 + HALF_ROPE:r0 + QK_ROPE_DIM, :] = (x1 * sin + x2 * cos).astype(BF16)
        qt_ref[0, hd, r0 + QK_ROPE_DIM:, :] = jnp.zeros((QK_PAD - r0 - QK_ROPE_DIM, tm), BF16)


def _qproj(h, g, wdq, qg, wq_t, cos_t, sin_t, *, tm=256):
    b, s, d = h.shape
    nh = wq_t.shape[0]
    rank = wdq.shape[1]
    tm = min(tm, s)
    return pl.pallas_call(
        _qproj_kernel,
        grid=(b, s // tm),
        in_specs=[
            pl.BlockSpec((1, tm, d), lambda bi, i: (bi, i, 0)),
            _const_spec((1, d)),
            _const_spec(wdq.shape),
            _const_spec((1, rank)),
            _const_spec(wq_t.shape),
            pl.BlockSpec((HALF_ROPE, tm), lambda bi, i: (0, i)),
            pl.BlockSpec((HALF_ROPE, tm), lambda bi, i: (0, i)),
        ],
        out_specs=pl.BlockSpec((1, nh, QK_PAD, tm), lambda bi, i: (bi, 0, 0, i)),
        out_shape=jax.ShapeDtypeStruct((b, nh, QK_PAD, s), BF16),
        compiler_params=_params(("arbitrary", "arbitrary")),
        name="qproj",
    )(h, g.reshape(1, d), wdq, qg.reshape(1, rank), wq_t, cos_t, sin_t)


def _attn_kernel(qt_ref, k_ref, vt_ref, o_ref, acc_ref, m_ref, s_ref, p_ref, cm_ref, *,
                 hb, tq, tk):
    qi = pl.program_id(2)
    kd = (qi * tq) // tk
    m_ref[...] = jnp.full(m_ref.shape, MASK_VALUE, F32)
    acc_ref[...] = jnp.zeros(acc_ref.shape, F32)

    def qk_head(k, slot, j):
        s = jnp.dot(k, qt_ref[0, j], preferred_element_type=F32)
        s_ref[slot, j] = s.astype(BF16)
        cm_ref[slot, j] = jnp.max(s, axis=0, keepdims=True)

    def softmax_head(slot, j, keep):
        s = s_ref[slot, j]
        if keep is None:
            cm = cm_ref[slot, j]
        else:
            s = jnp.where(keep, s, MASK_VALUE)
            cm = jnp.max(s, axis=0, keepdims=True).astype(F32)
        m_old = m_ref[j]
        m_new = jnp.maximum(m_old, cm).astype(BF16)
        m_ref[j] = m_new.astype(F32)
        p_ref[j] = jnp.exp2(s - m_new)
        return jnp.exp2(m_old - m_new.astype(F32))

    def pv_head(vt, j, alpha):
        pv = jnp.dot(vt, p_ref[j], preferred_element_type=F32)
        acc_ref[j] = alpha * acc_ref[j] + pv

    def stage(kt_qk, slot_qk, kt_sm, slot_sm, masked=False):
        k = None if kt_qk is None else k_ref[0, kt_qk]
        vt = vt_ref[0, kt_sm]
        keep = None
        if masked:
            kpos = kt_sm * tk + lax.broadcasted_iota(jnp.int32, (tk, tq), 0)
            qpos = qi * tq + lax.broadcasted_iota(jnp.int32, (tk, tq), 1)
            keep = kpos <= qpos
        for j in range(hb):
            if k is not None:
                qk_head(k, slot_qk, j)
            pv_head(vt, j, softmax_head(slot_sm, j, keep))

    k_diag = k_ref[0, kd]
    for j in range(hb):
        qk_head(k_diag, 0, j)
    stage(jnp.maximum(kd - 1, 0), 1, kd, 0, masked=True)

    def body(i, carry):
        idx = 2 * i + 1
        stage(kd - (idx + 1), 0, kd - idx, 1)
        stage(jnp.maximum(kd - (idx + 2), 0), 1, kd - (idx + 1), 0)
        return carry

    lax.fori_loop(0, kd // 2, body, 0)

    @pl.when(lax.rem(kd, 2) == 1)
    def _():
        stage(None, None, 0, 1)

    for j in range(hb):
        o = acc_ref[j, 0:V_DIM, :] / acc_ref[j, V_DIM:V_DIM + 1, :]
        o_ref[0, :, j * V_DIM:(j + 1) * V_DIM] = o.T.astype(o_ref.dtype)


def _attn(qt, k, vt, *, hb=16, tq=256):
    b, nh, _, s = qt.shape
    nkt, tk = k.shape[1], k.shape[2]
    assert tk % tq == 0 and vt.shape[2] == VT_ROWS
    return pl.pallas_call(
        functools.partial(_attn_kernel, hb=hb, tq=tq, tk=tk),
        grid=(b, nh // hb, s // tq),
        in_specs=[
            pl.BlockSpec((1, hb, QK_PAD, tq), lambda bi, hg, qi: (bi, hg, 0, qi)),
            pl.BlockSpec((1, nkt, tk, QK_PAD), lambda bi, hg, qi: (bi, 0, 0, 0),
                         pipeline_mode=pl.Buffered(1)),
            pl.BlockSpec((1, nkt, VT_ROWS, tk), lambda bi, hg, qi: (bi, 0, 0, 0),
                         pipeline_mode=pl.Buffered(1)),
        ],
        out_specs=pl.BlockSpec((1, tq, hb * V_DIM), lambda bi, hg, qi: (bi, qi, hg)),
        out_shape=jax.ShapeDtypeStruct((b, s, nh * V_DIM), BF16),
        scratch_shapes=[
            pltpu.VMEM((hb, VT_ROWS, tq), F32),
            pltpu.VMEM((hb, 1, tq), F32),
            pltpu.VMEM((2, hb, tk, tq), BF16),
            pltpu.VMEM((hb, tk, tq), BF16),
            pltpu.VMEM((2, hb, 1, tq), F32),
        ],
        compiler_params=_params(("arbitrary", "arbitrary", "arbitrary")),
        name="attn",
    )(qt, k, vt)


def _rope_tables(seq):
    pos = jnp.arange(seq, dtype=F32)
    inv_freq = ROPE_THETA ** (-jnp.arange(0, QK_ROPE_DIM, 2, dtype=F32) / QK_ROPE_DIM)
    ang = pos[:, None] * inv_freq[None, :]
    return jnp.cos(ang), jnp.sin(ang)


def kernel(x, ffn_pre_norm, ffn_pre_wg, ffn_pre_wu, ffn_pre_wd, mix_norm, ffn_post_norm, ffn_post_wg, ffn_post_wu, ffn_post_wd, pool_w, pool_scale, kv_in_norm, w_dkv, ckv_norm, w_uk, w_uv, q_lora_norm, w_dq, w_uq, w_o, final_norm):
    b, s, d = x.shape
    depth = ffn_pre_norm.shape[0]
    n_a = pool_w.shape[0]
    t = b * s
    tq = min(256, s)
    tk = min(512, s)

    pre = [w.astype(BF16) for w in (ffn_pre_wg, ffn_pre_wu, ffn_pre_wd)]
    post = [w.astype(BF16) for w in (ffn_post_wg, ffn_post_wu, ffn_post_wd)]
    pool_w_bf = pool_w.astype(BF16)

    cos, sin = _rope_tables(s)
    pad = jnp.zeros((s, LANES - QK_ROPE_DIM), F32)
    cos_k = jnp.concatenate([cos, cos, pad], axis=1)
    sin_k = jnp.concatenate([sin, sin, pad], axis=1)
    cos_t, sin_t = cos.T, sin.T

    wr = w_dkv[:, KV_LORA_RANK:]
    wpad = jnp.zeros((d, LANES - QK_ROPE_DIM), F32)
    w_kv3 = jnp.concatenate(
        [w_dkv[:, :KV_LORA_RANK], wr, wpad,
         -wr[:, HALF_ROPE:], wr[:, :HALF_ROPE], wpad], axis=1).astype(BF16)

    uk_h = jnp.transpose(w_uk, (1, 0, 2))
    uv_h = jnp.transpose(w_uv, (1, 0, 2))
    n_b = w_dq.shape[0]
    wq_ts, wo_abs = [], []
    for j in range(n_b):
        uq_h = jnp.transpose(w_uq[j], (1, 0, 2))
        a_nope = _absorb_q(uk_h, uq_h[:, :, :QK_NOPE_DIM])
        a_rope = jnp.transpose(uq_h[:, :, QK_NOPE_DIM:], (0, 2, 1))
        wq_ts.append(jnp.concatenate([a_nope, a_rope], axis=1).astype(BF16))
        wo_h = w_o[j].reshape(N_HEADS, V_DIM, d)
        wo_abs.append(_absorb_o(uv_h, wo_h).reshape(N_HEADS * KV_LORA_RANK, d))

    h = x.reshape(t, d)
    k_lat = vt_lat = None
    o_prev = wo_prev = None
    for l in range(depth):
        h = _ffn(h, ffn_pre_norm[l], pre[0][l], pre[1][l], pre[2][l])
        if l < n_a:
            h = _pool(h.reshape(b, s, d), mix_norm[l], pool_w_bf[l], pool_scale[l]).reshape(t, d)
            o_prev = wo_prev = None
        else:
            j = l - n_a
            qt = _qproj(h.reshape(b, s, d), mix_norm[l], w_dq[j].astype(BF16),
                        q_lora_norm[j], wq_ts[j], cos_t, sin_t, tm=tq)
            o_prev = _attn(qt, k_lat, vt_lat, tq=tq).reshape(t, N_HEADS * V_DIM)
            wo_prev = wo_abs[j]
        h = _ffn(h, ffn_post_norm[l], post[0][l], post[1][l], post[2][l],
                 o2d=o_prev, wo=wo_prev,
                 final_g=final_norm if l == depth - 1 else None)
        if l == n_a - 1:
            k_lat, vt_lat = _kv(h.reshape(b, s, d), kv_in_norm, w_kv3, ckv_norm,
                                cos_k, sin_k, tk=tk)
    return h.reshape(b, s, d)
```

```python
import functools

import jax
import jax.numpy as jnp
from jax import lax
from jax.experimental import pallas as pl
from jax.experimental.pallas import tpu as pltpu

RMS_EPS = 1e-6
POOL_WINDOWS = (2, 4, 8, 16)
MAX_WINDOW = max(POOL_WINDOWS)
N_HEADS = 16
QK_NOPE_DIM = 128
QK_ROPE_DIM = 64
QK_DIM = QK_NOPE_DIM + QK_ROPE_DIM
V_DIM = 128
KV_LORA_RANK = 128
ROPE_THETA = 10000.0
HALF_ROPE = QK_ROPE_DIM // 2

VMEM_LIMIT_BYTES = 56 * 1024 * 1024
LANES = 128
QK_PAD = 256
BF16_SUBLANES = 16
VT_ROWS = KV_LORA_RANK + BF16_SUBLANES
LOG2E = 1.4426950408889634
MASK_VALUE = -1e30

BF16 = jnp.bfloat16
F32 = jnp.float32


def _rmsnorm(x, g):
    ms = jnp.mean(x * x, axis=-1, keepdims=True)
    return x * lax.rsqrt(ms + RMS_EPS) * g


def _const_spec(shape):
    nd = len(shape)
    return pl.BlockSpec(shape, lambda *_: (0,) * nd, pipeline_mode=pl.Buffered(1))


def _layer_spec(stacked_shape, layer):
    rest = tuple(stacked_shape[1:])
    zeros = (0,) * len(rest)
    return pl.BlockSpec((None,) + rest, lambda *_: (layer,) + zeros,
                        pipeline_mode=pl.Buffered(1))


def _params(sem):
    return pltpu.CompilerParams(dimension_semantics=sem,
                                vmem_limit_bytes=VMEM_LIMIT_BYTES)


def _ffn_kernel(*refs, tf, with_oproj, with_final):
    it = iter(refs)
    h_ref = next(it)
    if with_oproj:
        o_ref = next(it)
        wo_ref = next(it)
    g_ref, wg_ref, wu_ref, wd_ref = next(it), next(it), next(it), next(it)
    if with_final:
        fg_ref = next(it)
    out_ref = next(it)
    a_ref = next(it)

    x = h_ref[...]
    if with_oproj:
        x = x + jnp.dot(o_ref[...], wo_ref[...], preferred_element_type=F32)
    u = _rmsnorm(x, g_ref[...]).astype(BF16)
    d_ff = wg_ref.shape[1]
    for c in range(d_ff // tf):
        sl = slice(c * tf, (c + 1) * tf)
        gate = jnp.dot(u, wg_ref[:, sl], preferred_element_type=F32)
        up = jnp.dot(u, wu_ref[:, sl], preferred_element_type=F32)
        a_ref[:, sl] = (gate * jax.nn.sigmoid(gate) * up).astype(BF16)
    y = jnp.dot(a_ref[...], wd_ref[...], preferred_element_type=F32)
    res = x + 0.5 * y
    if with_final:
        res = _rmsnorm(res, fg_ref[...])
    out_ref[...] = res


def _ffn(h2d, layer, g, wg, wu, wd, *, o2d=None, wo=None, final_g=None, tm=1024, tf=256):
    t, d = h2d.shape
    d_ff = wg.shape[2]
    tm = min(tm, t)
    row = lambda i: (i, 0)
    args = [h2d]
    specs = [pl.BlockSpec((tm, d), row)]
    if o2d is not None:
        args += [o2d, wo]
        specs += [pl.BlockSpec((tm, o2d.shape[1]), row), _const_spec(wo.shape)]
    args += [g, wg, wu, wd]
    specs += [_layer_spec(a.shape, layer) for a in (g, wg, wu, wd)]
    if final_g is not None:
        args.append(final_g.reshape(1, d))
        specs.append(_const_spec((1, d)))
    kern = functools.partial(_ffn_kernel, tf=tf, with_oproj=o2d is not None,
                             with_final=final_g is not None)
    return pl.pallas_call(
        kern,
        grid=(t // tm,),
        in_specs=specs,
        out_specs=pl.BlockSpec((tm, d), row),
        out_shape=jax.ShapeDtypeStruct((t, d), F32),
        scratch_shapes=[pltpu.VMEM((tm, d_ff), BF16)],
        compiler_params=_params(("arbitrary",)),
        name="ffn",
    )(*args)


def _pool_kernel(h_ref, halo_ref, g_ref, w_ref, sc_ref, out_ref, ext_ref, *, ts):
    i = pl.program_id(1)
    g = g_ref[...]
    x = h_ref[0]
    u = _rmsnorm(x, g)
    halo = _rmsnorm(halo_ref[0], g)
    ext_ref[0:MAX_WINDOW, :] = jnp.where(i > 0, halo, 0.0)
    ext_ref[MAX_WINDOW:, :] = u
    pos = i * ts + lax.broadcasted_iota(jnp.int32, (ts, 1), 0)
    dg = w_ref.shape[1]
    for gi, w in enumerate(POOL_WINDOWS):
        cs = slice(gi * dg, (gi + 1) * dg)
        win = ext_ref[:, cs]
        shift = 1
        while shift < w:
            win = win + pltpu.roll(win, shift, 0)
            shift *= 2
        win = win[MAX_WINDOW:]
        ug = u[:, cs]
        inv_count = 1.0 / jnp.minimum(pos + 1, w).astype(F32)
        y = (win * inv_count - ug).astype(BF16)
        z = jnp.dot(y, w_ref[gi], preferred_element_type=F32)
        out_ref[0, :, cs] = x[:, cs] + z * sc_ref[:, cs]


def _pool(h, g, w_groups, scale, *, ts=512):
    b, s, d = h.shape
    ts = min(ts, s)
    halo_blocks = ts // MAX_WINDOW
    return pl.pallas_call(
        functools.partial(_pool_kernel, ts=ts),
        grid=(b, s // ts),
        in_specs=[
            pl.BlockSpec((1, ts, d), lambda bi, i: (bi, i, 0)),
            pl.BlockSpec((1, MAX_WINDOW, d),
                         lambda bi, i: (bi, jnp.maximum(i * halo_blocks - 1, 0), 0)),
            _const_spec((1, d)),
            _const_spec(w_groups.shape),
            _const_spec((1, d)),
        ],
        out_specs=pl.BlockSpec((1, ts, d), lambda bi, i: (bi, i, 0)),
        out_shape=jax.ShapeDtypeStruct((b, s, d), F32),
        scratch_shapes=[pltpu.VMEM((ts + MAX_WINDOW, d), F32)],
        compiler_params=_params(("arbitrary", "arbitrary")),
        name="pool",
    )(h, h, g.reshape(1, d), w_groups, scale.reshape(1, d))


def _kv_kernel(h_ref, g_ref, w_ref, cg_ref, cos_ref, sin_ref, k_ref, vt_ref):
    u = _rmsnorm(h_ref[0], g_ref[...]).astype(BF16)
    kv = jnp.dot(u, w_ref[...], preferred_element_type=F32)
    c = _rmsnorm(kv[:, 0:LANES], cg_ref[...])
    rope = kv[:, LANES:2 * LANES] * cos_ref[...] + kv[:, 2 * LANES:] * sin_ref[...]
    k_ref[0, 0, :, 0:LANES] = c.astype(BF16)
    k_ref[0, 0, :, LANES:] = rope.astype(BF16)
    vt_ref[0, 0, 0:KV_LORA_RANK, :] = c.T.astype(BF16)
    extra = (BF16_SUBLANES, c.shape[0])
    ones_row = lax.broadcasted_iota(jnp.int32, extra, 0) == 0
    vt_ref[0, 0, KV_LORA_RANK:, :] = jnp.where(ones_row, 1.0, 0.0).astype(BF16)


def _kv(h, g, w_kv3, ckv_g, cos_k, sin_k, *, tk):
    b, s, d = h.shape
    nkt = s // tk
    return pl.pallas_call(
        _kv_kernel,
        grid=(b, nkt),
        in_specs=[
            pl.BlockSpec((1, tk, d), lambda bi, i: (bi, i, 0)),
            _const_spec((1, d)),
            _const_spec(w_kv3.shape),
            _const_spec((1, KV_LORA_RANK)),
            pl.BlockSpec((tk, LANES), lambda bi, i: (i, 0)),
            pl.BlockSpec((tk, LANES), lambda bi, i: (i, 0)),
        ],
        out_specs=[
            pl.BlockSpec((1, 1, tk, QK_PAD), lambda bi, i: (bi, i, 0, 0)),
            pl.BlockSpec((1, 1, VT_ROWS, tk), lambda bi, i: (bi, i, 0, 0)),
        ],
        out_shape=[
            jax.ShapeDtypeStruct((b, nkt, tk, QK_PAD), BF16),
            jax.ShapeDtypeStruct((b, nkt, VT_ROWS, tk), BF16),
        ],
        compiler_params=_params(("arbitrary", "arbitrary")),
        name="kv",
    )(h, g.reshape(1, d), w_kv3, ckv_g.reshape(1, KV_LORA_RANK), cos_k, sin_k)


def _absorb_q_kernel(uk_ref, uq_ref, out_ref):
    out_ref[0] = lax.dot_general(uk_ref[0], uq_ref[0], (((1,), (1,)), ((), ())),
                                 precision=lax.Precision.HIGHEST,
                                 preferred_element_type=F32)


def _absorb_o_kernel(uv_ref, wo_ref, out_ref):
    out_ref[0] = jnp.dot(uv_ref[0], wo_ref[0], precision=lax.Precision.HIGHEST,
                         preferred_element_type=F32).astype(out_ref.dtype)


def _absorb_q(uk_h, uq_h):
    nh, lat, _ = uk_h.shape
    rank = uq_h.shape[1]
    return pl.pallas_call(
        _absorb_q_kernel,
        grid=(nh,),
        in_specs=[pl.BlockSpec((1,) + uk_h.shape[1:], lambda h: (h, 0, 0)),
                  pl.BlockSpec((1,) + uq_h.shape[1:], lambda h: (h, 0, 0))],
        out_specs=pl.BlockSpec((1, lat, rank), lambda h: (h, 0, 0)),
        out_shape=jax.ShapeDtypeStruct((nh, lat, rank), F32),
        compiler_params=_params(("arbitrary",)),
        name="absorb_q",
    )(uk_h, uq_h)


def _absorb_o(uv_h, wo_h):
    nh, lat, _ = uv_h.shape
    n = wo_h.shape[2]
    return pl.pallas_call(
        _absorb_o_kernel,
        grid=(nh,),
        in_specs=[pl.BlockSpec((1,) + uv_h.shape[1:], lambda h: (h, 0, 0)),
                  pl.BlockSpec((1,) + wo_h.shape[1:], lambda h: (h, 0, 0))],
        out_specs=pl.BlockSpec((1, lat, n), lambda h: (h, 0, 0)),
        out_shape=jax.ShapeDtypeStruct((nh, lat, n), BF16),
        compiler_params=_params(("arbitrary",)),
        name="absorb_o",
    )(uv_h, wo_h)


def _qproj_kernel(h_ref, g_ref, wdq_ref, qg_ref, wq_ref, cos_ref, sin_ref, qt_ref):
    u = _rmsnorm(h_ref[0], g_ref[...]).astype(BF16)
    cq = _rmsnorm(jnp.dot(u, wdq_ref[...], preferred_element_type=F32),
                  qg_ref[...]).astype(BF16)
    cos = cos_ref[...]
    sin = sin_ref[...]
    scale = QK_DIM ** -0.5 * LOG2E
    tm = cq.shape[0]
    r0 = KV_LORA_RANK
    for hd in range(wq_ref.shape[0]):
        qt = lax.dot_general(wq_ref[hd], cq, (((1,), (1,)), ((), ())),
                             preferred_element_type=F32) * scale
        x1 = qt[r0:r0 + HALF_ROPE]
        x2 = qt[r0 + HALF_ROPE:r0 + QK_ROPE_DIM]
        qt_ref[0, hd, 0:r0, :] = qt[0:r0].astype(BF16)
        qt_ref[0, hd, r0:r0 + HALF_ROPE, :] = (x1 * cos - x2 * sin).astype(BF16)
        qt_ref[0, hd, r0 + HALF_ROPE:r0 + QK_ROPE_DIM, :] = (x1 * sin + x2 * cos).astype(BF16)
        qt_ref[0, hd, r0 + QK_ROPE_DIM:, :] = jnp.zeros((QK_PAD - r0 - QK_ROPE_DIM, tm), BF16)


def _qproj(h, g, wdq, qg, wq_t, cos_t, sin_t, *, tm=256):
    b, s, d = h.shape
    nh = wq_t.shape[0]
    rank = wdq.shape[1]
    tm = min(tm, s)
    return pl.pallas_call(
        _qproj_kernel,
        grid=(b, s // tm),
        in_specs=[
            pl.BlockSpec((1, tm, d), lambda bi, i: (bi, i, 0)),
            _const_spec((1, d)),
            _const_spec(wdq.shape),
            _const_spec((1, rank)),
            _const_spec(wq_t.shape),
            pl.BlockSpec((HALF_ROPE, tm), lambda bi, i: (0, i)),
            pl.BlockSpec((HALF_ROPE, tm), lambda bi, i: (0, i)),
        ],
        out_specs=pl.BlockSpec((1, nh, QK_PAD, tm), lambda bi, i: (bi, 0, 0, i)),
        out_shape=jax.ShapeDtypeStruct((b, nh, QK_PAD, s), BF16),
        compiler_params=_params(("arbitrary", "arbitrary")),
        name="qproj",
    )(h, g.reshape(1, d), wdq, qg.reshape(1, rank), wq_t, cos_t, sin_t)


def _attn_kernel(qt_ref, k_ref, vt_ref, o_ref, acc_ref, m_ref, s_ref, p_ref, cm_ref, *,
                 hb, tq, tk):
    qi = pl.program_id(2)
    kd = (qi * tq) // tk
    m_ref[...] = jnp.full(m_ref.shape, MASK_VALUE, F32)
    acc_ref[...] = jnp.zeros(acc_ref.shape, F32)

    def qk_head(k, slot, j):
        s = jnp.dot(k, qt_ref[0, j], preferred_element_type=F32)
        s_ref[slot, j] = s.astype(BF16)
        cm_ref[slot, j] = jnp.max(s, axis=0, keepdims=True)

    def softmax_head(slot, j, keep):
        s = s_ref[slot, j]
        if keep is None:
            cm = cm_ref[slot, j]
        else:
            s = jnp.where(keep, s, MASK_VALUE)
            cm = jnp.max(s, axis=0, keepdims=True).astype(F32)
        m_old = m_ref[j]
        m_new = jnp.maximum(m_old, cm).astype(BF16)
        m_ref[j] = m_new.astype(F32)
        p_ref[j] = jnp.exp2(s - m_new)
        return jnp.exp2(m_old - m_new.astype(F32))

    def pv_head(vt, j, alpha):
        pv = jnp.dot(vt, p_ref[j], preferred_element_type=F32)
        acc_ref[j] = alpha * acc_ref[j] + pv

    def stage(kt_qk, slot_qk, kt_sm, slot_sm, masked=False):
        k = None if kt_qk is None else k_ref[0, kt_qk]
        vt = vt_ref[0, kt_sm]
        keep = None
        if masked:
            kpos = kt_sm * tk + lax.broadcasted_iota(jnp.int32, (tk, tq), 0)
            qpos = qi * tq + lax.broadcasted_iota(jnp.int32, (tk, tq), 1)
            keep = kpos <= qpos
        for j in range(hb):
            if k is not None:
                qk_head(k, slot_qk, j)
            pv_head(vt, j, softmax_head(slot_sm, j, keep))

    k_diag = k_ref[0, kd]
    for j in range(hb):
        qk_head(k_diag, 0, j)
    stage(jnp.maximum(kd - 1, 0), 1, kd, 0, masked=True)

    def body(i, carry):
        idx = 2 * i + 1
        stage(kd - (idx + 1), 0, kd - idx, 1)
        stage(jnp.maximum(kd - (idx + 2), 0), 1, kd - (idx + 1), 0)
        return carry

    lax.fori_loop(0, kd // 2, body, 0)

    @pl.when(lax.rem(kd, 2) == 1)
    def _():
        stage(None, None, 0, 1)

    for j in range(hb):
        o = acc_ref[j, 0:V_DIM, :] / acc_ref[j, V_DIM:V_DIM + 1, :]
        o_ref[0, :, j * V_DIM:(j + 1) * V_DIM] = o.T.astype(o_ref.dtype)


def _attn(qt, k, vt, *, hb=16, tq=256):
    b, nh, _, s = qt.shape
    nkt, tk = k.shape[1], k.shape[2]
    assert tk % tq == 0 and vt.shape[2] == VT_ROWS
    return pl.pallas_call(
        functools.partial(_attn_kernel, hb=hb, tq=tq, tk=tk),
        grid=(b, nh // hb, s // tq),
        in_specs=[
            pl.BlockSpec((1, hb, QK_PAD, tq), lambda bi, hg, qi: (bi, hg, 0, qi)),
            pl.BlockSpec((1, nkt, tk, QK_PAD), lambda bi, hg, qi: (bi, 0, 0, 0),
                         pipeline_mode=pl.Buffered(1)),
            pl.BlockSpec((1, nkt, VT_ROWS, tk), lambda bi, hg, qi: (bi, 0, 0, 0),
                         pipeline_mode=pl.Buffered(1)),
        ],
        out_specs=pl.BlockSpec((1, tq, hb * V_DIM), lambda bi, hg, qi: (bi, qi, hg)),
        out_shape=jax.ShapeDtypeStruct((b, s, nh * V_DIM), BF16),
        scratch_shapes=[
            pltpu.VMEM((hb, VT_ROWS, tq), F32),
            pltpu.VMEM((hb, 1, tq), F32),
            pltpu.VMEM((2, hb, tk, tq), BF16),
            pltpu.VMEM((hb, tk, tq), BF16),
            pltpu.VMEM((2, hb, 1, tq), F32),
        ],
        compiler_params=_params(("arbitrary", "arbitrary", "arbitrary")),
        name="attn",
    )(qt, k, vt)


def _rope_tables(seq):
    pos = jnp.arange(seq, dtype=F32)
    inv_freq = ROPE_THETA ** (-jnp.arange(0, QK_ROPE_DIM, 2, dtype=F32) / QK_ROPE_DIM)
    ang = pos[:, None] * inv_freq[None, :]
    return jnp.cos(ang), jnp.sin(ang)


def kernel(x, ffn_pre_norm, ffn_pre_wg, ffn_pre_wu, ffn_pre_wd, mix_norm, ffn_post_norm, ffn_post_wg, ffn_post_wu, ffn_post_wd, pool_w, pool_scale, kv_in_norm, w_dkv, ckv_norm, w_uk, w_uv, q_lora_norm, w_dq, w_uq, w_o, final_norm):
    b, s, d = x.shape
    depth = ffn_pre_norm.shape[0]
    n_a = pool_w.shape[0]
    t = b * s
    tq = min(256, s)
    tk = min(512, s)

    pre = [w.astype(BF16) for w in (ffn_pre_wg, ffn_pre_wu, ffn_pre_wd)]
    post = [w.astype(BF16) for w in (ffn_post_wg, ffn_post_wu, ffn_post_wd)]
    pre_g = ffn_pre_norm.reshape(depth, 1, d)
    post_g = ffn_post_norm.reshape(depth, 1, d)
    pool_w_bf = pool_w.astype(BF16)

    cos, sin = _rope_tables(s)
    pad = jnp.zeros((s, LANES - QK_ROPE_DIM), F32)
    cos_k = jnp.concatenate([cos, cos, pad], axis=1)
    sin_k = jnp.concatenate([sin, sin, pad], axis=1)
    cos_t, sin_t = cos.T, sin.T

    wr = w_dkv[:, KV_LORA_RANK:]
    wpad = jnp.zeros((d, LANES - QK_ROPE_DIM), F32)
    w_kv3 = jnp.concatenate(
        [w_dkv[:, :KV_LORA_RANK], wr, wpad,
         -wr[:, HALF_ROPE:], wr[:, :HALF_ROPE], wpad], axis=1).astype(BF16)

    uk_h = jnp.transpose(w_uk, (1, 0, 2))
    uv_h = jnp.transpose(w_uv, (1, 0, 2))
    n_b = w_dq.shape[0]
    wq_ts, wo_abs = [], []
    for j in range(n_b):
        uq_h = jnp.transpose(w_uq[j], (1, 0, 2))
        a_nope = _absorb_q(uk_h, uq_h[:, :, :QK_NOPE_DIM])
        a_rope = jnp.transpose(uq_h[:, :, QK_NOPE_DIM:], (0, 2, 1))
        wq_ts.append(jnp.concatenate([a_nope, a_rope], axis=1).astype(BF16))
        wo_h = w_o[j].reshape(N_HEADS, V_DIM, d)
        wo_abs.append(_absorb_o(uv_h, wo_h).reshape(N_HEADS * KV_LORA_RANK, d))

    h = x.reshape(t, d)
    k_lat = vt_lat = None
    o_prev = wo_prev = None
    for l in range(depth):
        h = _ffn(h, l, pre_g, *pre)
        if l < n_a:
            h = _pool(h.reshape(b, s, d), mix_norm[l], pool_w_bf[l], pool_scale[l]).reshape(t, d)
            o_prev = wo_prev = None
        else:
            j = l - n_a
            qt = _qproj(h.reshape(b, s, d), mix_norm[l], w_dq[j].astype(BF16),
                        q_lora_norm[j], wq_ts[j], cos_t, sin_t, tm=tq)
            o_prev = _attn(qt, k_lat, vt_lat, tq=tq, hb=(16, 8)[j % 2]).reshape(t, N_HEADS * V_DIM)
            wo_prev = wo_abs[j]
        h = _ffn(h, l, post_g, *post, o2d=o_prev, wo=wo_prev,
                 final_g=final_norm if l == depth - 1 else None)
        if l == n_a - 1:
            k_lat, vt_lat = _kv(h.reshape(b, s, d), kv_in_norm, w_kv3, ckv_norm,
                                cos_k, sin_k, tk=tk)
    return h.reshape(b, s, d)
```

```python
import functools

import jax
import jax.numpy as jnp
from jax import lax
from jax.experimental import pallas as pl
from jax.experimental.pallas import tpu as pltpu

RMS_EPS = 1e-6
POOL_WINDOWS = (2, 4, 8, 16)
MAX_WINDOW = max(POOL_WINDOWS)
N_HEADS = 16
QK_NOPE_DIM = 128
QK_ROPE_DIM = 64
QK_DIM = QK_NOPE_DIM + QK_ROPE_DIM
V_DIM = 128
KV_LORA_RANK = 128
ROPE_THETA = 10000.0
HALF_ROPE = QK_ROPE_DIM // 2

VMEM_LIMIT_BYTES = 56 * 1024 * 1024
LANES = 128
QK_PAD = 256
BF16_SUBLANES = 16
VT_ROWS = KV_LORA_RANK + BF16_SUBLANES
LOG2E = 1.4426950408889634
MASK_VALUE = -1e30

BF16 = jnp.bfloat16
F32 = jnp.float32


def _rmsnorm(x, g):
    ms = jnp.mean(x * x, axis=-1, keepdims=True)
    return x * lax.rsqrt(ms + RMS_EPS) * g


def _const_spec(shape):
    nd = len(shape)
    return pl.BlockSpec(shape, lambda *_: (0,) * nd, pipeline_mode=pl.Buffered(1))


def _layer_spec(stacked_shape, layer):
    rest = tuple(stacked_shape[1:])
    zeros = (0,) * len(rest)
    return pl.BlockSpec((None,) + rest, lambda *_: (layer,) + zeros,
                        pipeline_mode=pl.Buffered(1))


def _params(sem):
    return pltpu.CompilerParams(dimension_semantics=sem,
                                vmem_limit_bytes=VMEM_LIMIT_BYTES)


def _ffn_kernel(*refs, tf, with_oproj, with_final):
    it = iter(refs)
    h_ref = next(it)
    if with_oproj:
        o_ref = next(it)
        wo_ref = next(it)
    g_ref, wg_ref, wu_ref, wd_ref = next(it), next(it), next(it), next(it)
    if with_final:
        fg_ref = next(it)
    out_ref = next(it)
    a_ref = next(it)

    x = h_ref[...]
    if with_oproj:
        x = x + jnp.dot(o_ref[...], wo_ref[...], preferred_element_type=F32)
    u = _rmsnorm(x, g_ref[...]).astype(BF16)
    d_ff = wg_ref.shape[1]
    for c in range(d_ff // tf):
        sl = slice(c * tf, (c + 1) * tf)
        gate = jnp.dot(u, wg_ref[:, sl], preferred_element_type=F32)
        up = jnp.dot(u, wu_ref[:, sl], preferred_element_type=F32)
        a_ref[:, sl] = (gate * jax.nn.sigmoid(gate) * up).astype(BF16)
    y = jnp.dot(a_ref[...], wd_ref[...], preferred_element_type=F32)
    res = x + 0.5 * y
    if with_final:
        res = _rmsnorm(res, fg_ref[...])
    out_ref[...] = res


def _ffn(h2d, layer, g, wg, wu, wd, *, o2d=None, wo=None, final_g=None, tm=1024, tf=256):
    t, d = h2d.shape
    d_ff = wg.shape[2]
    tm = min(tm, t)
    row = lambda i: (i, 0)
    args = [h2d]
    specs = [pl.BlockSpec((tm, d), row)]
    if o2d is not None:
        args += [o2d, wo]
        specs += [pl.BlockSpec((tm, o2d.shape[1]), row), _const_spec(wo.shape)]
    args += [g, wg, wu, wd]
    specs += [_layer_spec(a.shape, layer) for a in (g, wg, wu, wd)]
    if final_g is not None:
        args.append(final_g.reshape(1, d))
        specs.append(_const_spec((1, d)))
    kern = functools.partial(_ffn_kernel, tf=tf, with_oproj=o2d is not None,
                             with_final=final_g is not None)
    return pl.pallas_call(
        kern,
        grid=(t // tm,),
        in_specs=specs,
        out_specs=pl.BlockSpec((tm, d), row),
        out_shape=jax.ShapeDtypeStruct((t, d), F32),
        scratch_shapes=[pltpu.VMEM((tm, d_ff), BF16)],
        compiler_params=_params(("arbitrary",)),
        name="ffn",
    )(*args)


def _pool_kernel(h_ref, halo_ref, g_ref, w_ref, sc_ref, out_ref, ext_ref, *, ts):
    i = pl.program_id(1)
    g = g_ref[...]
    x = h_ref[0]
    u = _rmsnorm(x, g)
    halo = _rmsnorm(halo_ref[0], g)
    ext_ref[0:MAX_WINDOW, :] = jnp.where(i > 0, halo, 0.0)
    ext_ref[MAX_WINDOW:, :] = u
    pos = i * ts + lax.broadcasted_iota(jnp.int32, (ts, 1), 0)
    dg = w_ref.shape[1]
    for gi, w in enumerate(POOL_WINDOWS):
        cs = slice(gi * dg, (gi + 1) * dg)
        win = ext_ref[:, cs]
        shift = 1
        while shift < w:
            win = win + pltpu.roll(win, shift, 0)
            shift *= 2
        win = win[MAX_WINDOW:]
        ug = u[:, cs]
        inv_count = 1.0 / jnp.minimum(pos + 1, w).astype(F32)
        y = (win * inv_count - ug).astype(BF16)
        z = jnp.dot(y, w_ref[gi], preferred_element_type=F32)
        out_ref[0, :, cs] = x[:, cs] + z * sc_ref[:, cs]


def _pool(h, g, w_groups, scale, *, ts=512):
    b, s, d = h.shape
    ts = min(ts, s)
    halo_blocks = ts // MAX_WINDOW
    return pl.pallas_call(
        functools.partial(_pool_kernel, ts=ts),
        grid=(b, s // ts),
        in_specs=[
            pl.BlockSpec((1, ts, d), lambda bi, i: (bi, i, 0)),
            pl.BlockSpec((1, MAX_WINDOW, d),
                         lambda bi, i: (bi, jnp.maximum(i * halo_blocks - 1, 0), 0)),
            _const_spec((1, d)),
            _const_spec(w_groups.shape),
            _const_spec((1, d)),
        ],
        out_specs=pl.BlockSpec((1, ts, d), lambda bi, i: (bi, i, 0)),
        out_shape=jax.ShapeDtypeStruct((b, s, d), F32),
        scratch_shapes=[pltpu.VMEM((ts + MAX_WINDOW, d), F32)],
        compiler_params=_params(("arbitrary", "arbitrary")),
        name="pool",
    )(h, h, g.reshape(1, d), w_groups, scale.reshape(1, d))


def _kv_kernel(h_ref, g_ref, w_ref, cg_ref, cos_ref, sin_ref, k_ref, vt_ref):
    u = _rmsnorm(h_ref[0], g_ref[...]).astype(BF16)
    kv = jnp.dot(u, w_ref[...], preferred_element_type=F32)
    c = _rmsnorm(kv[:, 0:LANES], cg_ref[...])
    rope = kv[:, LANES:2 * LANES] * cos_ref[...] + kv[:, 2 * LANES:] * sin_ref[...]
    k_ref[0, 0, :, 0:LANES] = c.astype(BF16)
    k_ref[0, 0, :, LANES:] = rope.astype(BF16)
    vt_ref[0, 0, 0:KV_LORA_RANK, :] = c.T.astype(BF16)
    extra = (BF16_SUBLANES, c.shape[0])
    ones_row = lax.broadcasted_iota(jnp.int32, extra, 0) == 0
    vt_ref[0, 0, KV_LORA_RANK:, :] = jnp.where(ones_row, 1.0, 0.0).astype(BF16)


def _kv(h, g, w_kv3, ckv_g, cos_k, sin_k, *, tk):
    b, s, d = h.shape
    nkt = s // tk
    return pl.pallas_call(
        _kv_kernel,
        grid=(b, nkt),
        in_specs=[
            pl.BlockSpec((1, tk, d), lambda bi, i: (bi, i, 0)),
            _const_spec((1, d)),
            _const_spec(w_kv3.shape),
            _const_spec((1, KV_LORA_RANK)),
            pl.BlockSpec((tk, LANES), lambda bi, i: (i, 0)),
            pl.BlockSpec((tk, LANES), lambda bi, i: (i, 0)),
        ],
        out_specs=[
            pl.BlockSpec((1, 1, tk, QK_PAD), lambda bi, i: (bi, i, 0, 0)),
            pl.BlockSpec((1, 1, VT_ROWS, tk), lambda bi, i: (bi, i, 0, 0)),
        ],
        out_shape=[
            jax.ShapeDtypeStruct((b, nkt, tk, QK_PAD), BF16),
            jax.ShapeDtypeStruct((b, nkt, VT_ROWS, tk), BF16),
        ],
        compiler_params=_params(("arbitrary", "arbitrary")),
        name="kv",
    )(h, g.reshape(1, d), w_kv3, ckv_g.reshape(1, KV_LORA_RANK), cos_k, sin_k)


def _absorb_q_kernel(uk_ref, uq_ref, out_ref):
    out_ref[0] = lax.dot_general(uk_ref[0], uq_ref[0], (((1,), (1,)), ((), ())),
                                 precision=lax.Precision.HIGHEST,
                                 preferred_element_type=F32)


def _absorb_o_kernel(uv_ref, wo_ref, out_ref):
    out_ref[0] = jnp.dot(uv_ref[0], wo_ref[0], precision=lax.Precision.HIGHEST,
                         preferred_element_type=F32).astype(out_ref.dtype)


def _absorb_q(uk_h, uq_h):
    nh, lat, _ = uk_h.shape
    rank = uq_h.shape[1]
    return pl.pallas_call(
        _absorb_q_kernel,
        grid=(nh,),
        in_specs=[pl.BlockSpec((1,) + uk_h.shape[1:], lambda h: (h, 0, 0)),
                  pl.BlockSpec((1,) + uq_h.shape[1:], lambda h: (h, 0, 0))],
        out_specs=pl.BlockSpec((1, lat, rank), lambda h: (h, 0, 0)),
        out_shape=jax.ShapeDtypeStruct((nh, lat, rank), F32),
        compiler_params=_params(("arbitrary",)),
        name="absorb_q",
    )(uk_h, uq_h)


def _absorb_o(uv_h, wo_h):
    nh, lat, _ = uv_h.shape
    n = wo_h.shape[2]
    return pl.pallas_call(
        _absorb_o_kernel,
        grid=(nh,),
        in_specs=[pl.BlockSpec((1,) + uv_h.shape[1:], lambda h: (h, 0, 0)),
                  pl.BlockSpec((1,) + wo_h.shape[1:], lambda h: (h, 0, 0))],
        out_specs=pl.BlockSpec((1, lat, n), lambda h: (h, 0, 0)),
        out_shape=jax.ShapeDtypeStruct((nh, lat, n), BF16),
        compiler_params=_params(("arbitrary",)),
        name="absorb_o",
    )(uv_h, wo_h)


def _qproj_kernel(h_ref, g_ref, wdq_ref, qg_ref, wq_ref, cos_ref, sin_ref, qt_ref):
    u = _rmsnorm(h_ref[0], g_ref[...]).astype(BF16)
    scale = QK_DIM ** -0.5 * LOG2E
    cq = (_rmsnorm(jnp.dot(u, wdq_ref[...], preferred_element_type=F32),
                   qg_ref[...]) * scale).astype(BF16)
    cos = cos_ref[...]
    sin = sin_ref[...]
    tm = cq.shape[0]
    r0 = KV_LORA_RANK
    for hd in range(wq_ref.shape[0]):
        qt = lax.dot_general(wq_ref[hd], cq, (((1,), (1,)), ((), ())),
                             preferred_element_type=F32)
        x1 = qt[r0:r0 + HALF_ROPE]
        x2 = qt[r0 + HALF_ROPE:r0 + QK_ROPE_DIM]
        qt_ref[0, hd, 0:r0, :] = qt[0:r0].astype(BF16)
        qt_ref[0, hd, r0:r0 + HALF_ROPE, :] = (x1 * cos - x2 * sin).astype(BF16)
        qt_ref[0, hd, r0 + HALF_ROPE:r0 + QK_ROPE_DIM, :] = (x1 * sin + x2 * cos).astype(BF16)
        qt_ref[0, hd, r0 + QK_ROPE_DIM:, :] = jnp.zeros((QK_PAD - r0 - QK_ROPE_DIM, tm), BF16)


def _qproj(h, g, wdq, qg, wq_t, cos_t, sin_t, *, tm=256):
    b, s, d = h.shape
    nh = wq_t.shape[0]
    rank = wdq.shape[1]
    tm = min(tm, s)
    return pl.pallas_call(
        _qproj_kernel,
        grid=(b, s // tm),
        in_specs=[
            pl.BlockSpec((1, tm, d), lambda bi, i: (bi, i, 0)),
            _const_spec((1, d)),
            _const_spec(wdq.shape),
            _const_spec((1, rank)),
            _const_spec(wq_t.shape),
            pl.BlockSpec((HALF_ROPE, tm), lambda bi, i: (0, i)),
            pl.BlockSpec((HALF_ROPE, tm), lambda bi, i: (0, i)),
        ],
        out_specs=pl.BlockSpec((1, nh, QK_PAD, tm), lambda bi, i: (bi, 0, 0, i)),
        out_shape=jax.ShapeDtypeStruct((b, nh, QK_PAD, s), BF16),
        compiler_params=_params(("arbitrary", "arbitrary")),
        name="qproj",
    )(h, g.reshape(1, d), wdq, qg.reshape(1, rank), wq_t, cos_t, sin_t)


def _attn_kernel(qt_ref, k_ref, vt_ref, o_ref, acc_ref, m_ref, s_ref, p_ref, cm_ref, *,
                 hb, tq, tk):
    qi = pl.program_id(2)
    kd = (qi * tq) // tk
    m_ref[...] = jnp.full(m_ref.shape, MASK_VALUE, F32)
    acc_ref[...] = jnp.zeros(acc_ref.shape, F32)

    def qk_head(k, slot, j):
        s = jnp.dot(k, qt_ref[0, j], preferred_element_type=F32)
        s_ref[slot, j] = s.astype(BF16)
        cm_ref[slot, j] = jnp.max(s, axis=0, keepdims=True)

    def softmax_head(slot, j, keep):
        s = s_ref[slot, j]
        if keep is None:
            cm = cm_ref[slot, j]
        else:
            s = jnp.where(keep, s, MASK_VALUE)
            cm = jnp.max(s, axis=0, keepdims=True).astype(F32)
        m_old = m_ref[j]
        m_new = jnp.maximum(m_old, cm).astype(BF16)
        m_ref[j] = m_new.astype(F32)
        p_ref[j] = jnp.exp2(s - m_new)
        return jnp.exp2(m_old - m_new.astype(F32))

    def pv_head(vt, j, alpha):
        pv = jnp.dot(vt, p_ref[j], preferred_element_type=F32)
        acc_ref[j] = alpha * acc_ref[j] + pv

    def stage(kt_qk, slot_qk, kt_sm, slot_sm, masked=False):
        k = None if kt_qk is None else k_ref[0, kt_qk]
        vt = vt_ref[0, kt_sm]
        keep = None
        if masked:
            kpos = kt_sm * tk + lax.broadcasted_iota(jnp.int32, (tk, tq), 0)
            qpos = qi * tq + lax.broadcasted_iota(jnp.int32, (tk, tq), 1)
            keep = kpos <= qpos
        for j in range(hb):
            if k is not None:
                qk_head(k, slot_qk, j)
            pv_head(vt, j, softmax_head(slot_sm, j, keep))

    k_diag = k_ref[0, kd]
    for j in range(hb):
        qk_head(k_diag, 0, j)
    stage(jnp.maximum(kd - 1, 0), 1, kd, 0, masked=True)

    def body(i, carry):
        idx = 2 * i + 1
        stage(kd - (idx + 1), 0, kd - idx, 1)
        stage(jnp.maximum(kd - (idx + 2), 0), 1, kd - (idx + 1), 0)
        return carry

    lax.fori_loop(0, kd // 2, body, 0)

    @pl.when(lax.rem(kd, 2) == 1)
    def _():
        stage(None, None, 0, 1)

    for j in range(hb):
        o = acc_ref[j, 0:V_DIM, :] * (1.0 / acc_ref[j, V_DIM:V_DIM + 1, :])
        o_ref[0, :, j * V_DIM:(j + 1) * V_DIM] = o.T.astype(o_ref.dtype)


def _attn(qt, k, vt, *, hb=16, tq=256):
    b, nh, _, s = qt.shape
    nkt, tk = k.shape[1], k.shape[2]
    assert tk % tq == 0 and vt.shape[2] == VT_ROWS
    return pl.pallas_call(
        functools.partial(_attn_kernel, hb=hb, tq=tq, tk=tk),
        grid=(b, nh // hb, s // tq),
        in_specs=[
            pl.BlockSpec((1, hb, QK_PAD, tq), lambda bi, hg, qi: (bi, hg, 0, qi)),
            pl.BlockSpec((1, nkt, tk, QK_PAD), lambda bi, hg, qi: (bi, 0, 0, 0),
                         pipeline_mode=pl.Buffered(1)),
            pl.BlockSpec((1, nkt, VT_ROWS, tk), lambda bi, hg, qi: (bi, 0, 0, 0),
                         pipeline_mode=pl.Buffered(1)),
        ],
        out_specs=pl.BlockSpec((1, tq, hb * V_DIM), lambda bi, hg, qi: (bi, qi, hg)),
        out_shape=jax.ShapeDtypeStruct((b, s, nh * V_DIM), BF16),
        scratch_shapes=[
            pltpu.VMEM((hb, VT_ROWS, tq), F32),
            pltpu.VMEM((hb, 1, tq), F32),
            pltpu.VMEM((2, hb, tk, tq), BF16),
            pltpu.VMEM((hb, tk, tq), BF16),
            pltpu.VMEM((2, hb, 1, tq), F32),
        ],
        compiler_params=_params(("arbitrary", "arbitrary", "arbitrary")),
        name="attn",
    )(qt, k, vt)


def _rope_tables(seq):
    pos = jnp.arange(seq, dtype=F32)
    inv_freq = ROPE_THETA ** (-jnp.arange(0, QK_ROPE_DIM, 2, dtype=F32) / QK_ROPE_DIM)
    ang = pos[:, None] * inv_freq[None, :]
    return jnp.cos(ang), jnp.sin(ang)


def kernel(x, ffn_pre_norm, ffn_pre_wg, ffn_pre_wu, ffn_pre_wd, mix_norm, ffn_post_norm, ffn_post_wg, ffn_post_wu, ffn_post_wd, pool_w, pool_scale, kv_in_norm, w_dkv, ckv_norm, w_uk, w_uv, q_lora_norm, w_dq, w_uq, w_o, final_norm):
    b, s, d = x.shape
    depth = ffn_pre_norm.shape[0]
    n_a = pool_w.shape[0]
    t = b * s
    tq = min(256, s)
    tk = min(512, s)

    pre = [w.astype(BF16) for w in (ffn_pre_wg, ffn_pre_wu, ffn_pre_wd)]
    post = [w.astype(BF16) for w in (ffn_post_wg, ffn_post_wu, ffn_post_wd)]
    pre_g = ffn_pre_norm.reshape(depth, 1, d)
    post_g = ffn_post_norm.reshape(depth, 1, d)
    pool_w_bf = pool_w.astype(BF16)

    cos, sin = _rope_tables(s)
    pad = jnp.zeros((s, LANES - QK_ROPE_DIM), F32)
    cos_k = jnp.concatenate([cos, cos, pad], axis=1)
    sin_k = jnp.concatenate([sin, sin, pad], axis=1)
    cos_t, sin_t = cos.T, sin.T

    wr = w_dkv[:, KV_LORA_RANK:]
    wpad = jnp.zeros((d, LANES - QK_ROPE_DIM), F32)
    w_kv3 = jnp.concatenate(
        [w_dkv[:, :KV_LORA_RANK], wr, wpad,
         -wr[:, HALF_ROPE:], wr[:, :HALF_ROPE], wpad], axis=1).astype(BF16)

    uk_h = jnp.transpose(w_uk, (1, 0, 2))
    uv_h = jnp.transpose(w_uv, (1, 0, 2))
    n_b = w_dq.shape[0]
    wq_ts, wo_abs = [], []
    for j in range(n_b):
        uq_h = jnp.transpose(w_uq[j], (1, 0, 2))
        a_nope = _absorb_q(uk_h, uq_h[:, :, :QK_NOPE_DIM])
        a_rope = jnp.transpose(uq_h[:, :, QK_NOPE_DIM:], (0, 2, 1))
        wq_ts.append(jnp.concatenate([a_nope, a_rope], axis=1).astype(BF16))
        wo_h = w_o[j].reshape(N_HEADS, V_DIM, d)
        wo_abs.append(_absorb_o(uv_h, wo_h).reshape(N_HEADS * KV_LORA_RANK, d))

    h = x.reshape(t, d)
    k_lat = vt_lat = None
    o_prev = wo_prev = None
    for l in range(depth):
        h = _ffn(h, l, pre_g, *pre)
        if l < n_a:
            h = _pool(h.reshape(b, s, d), mix_norm[l], pool_w_bf[l], pool_scale[l]).reshape(t, d)
            o_prev = wo_prev = None
        else:
            j = l - n_a
            qt = _qproj(h.reshape(b, s, d), mix_norm[l], w_dq[j].astype(BF16),
                        q_lora_norm[j], wq_ts[j], cos_t, sin_t, tm=tq)
            o_prev = _attn(qt, k_lat, vt_lat, tq=tq).reshape(t, N_HEADS * V_DIM)
            wo_prev = wo_abs[j]
        h = _ffn(h, l, post_g, *post, o2d=o_prev, wo=wo_prev,
                 final_g=final_norm if l == depth - 1 else None)
        if l == n_a - 1:
            k_lat, vt_lat = _kv(h.reshape(b, s, d), kv_in_norm, w_kv3, ckv_norm,
                                cos_k, sin_k, tk=tk)
    return h.reshape(b, s, d)
```

```python
import functools

import jax
import jax.numpy as jnp
from jax import lax
from jax.experimental import pallas as pl
from jax.experimental.pallas import tpu as pltpu

RMS_EPS = 1e-6
POOL_WINDOWS = (2, 4, 8, 16)
MAX_WINDOW = max(POOL_WINDOWS)
N_HEADS = 16
QK_NOPE_DIM = 128
QK_ROPE_DIM = 64
QK_DIM = QK_NOPE_DIM + QK_ROPE_DIM
V_DIM = 128
KV_LORA_RANK = 128
ROPE_THETA = 10000.0
HALF_ROPE = QK_ROPE_DIM // 2

VMEM_LIMIT_BYTES = 56 * 1024 * 1024
LANES = 128
QK_PAD = 256
BF16_SUBLANES = 16
VT_ROWS = KV_LORA_RANK + BF16_SUBLANES
LOG2E = 1.4426950408889634
MASK_VALUE = -1e30

BF16 = jnp.bfloat16
F32 = jnp.float32


def _rmsnorm(x, g):
    ms = jnp.mean(x * x, axis=-1, keepdims=True)
    return x * lax.rsqrt(ms + RMS_EPS) * g


def _const_spec(shape):
    nd = len(shape)
    return pl.BlockSpec(shape, lambda *_: (0,) * nd, pipeline_mode=pl.Buffered(1))


def _layer_spec(stacked_shape, layer):
    rest = tuple(stacked_shape[1:])
    zeros = (0,) * len(rest)
    return pl.BlockSpec((None,) + rest, lambda *_: (layer,) + zeros,
                        pipeline_mode=pl.Buffered(1))


def _params(sem):
    return pltpu.CompilerParams(dimension_semantics=sem,
                                vmem_limit_bytes=VMEM_LIMIT_BYTES)


def _ffn_kernel(*refs, tf, with_oproj, with_final):
    it = iter(refs)
    h_ref = next(it)
    if with_oproj:
        o_ref = next(it)
        wo_ref = next(it)
    g_ref, wg_ref, wu_ref, wd_ref = next(it), next(it), next(it), next(it)
    if with_final:
        fg_ref = next(it)
    out_ref = next(it)
    a_ref = next(it)

    x = h_ref[...]
    if with_oproj:
        x = x + jnp.dot(o_ref[...], wo_ref[...], preferred_element_type=F32)
    u = _rmsnorm(x, g_ref[...]).astype(BF16)
    d_ff = wg_ref.shape[1]
    for c in range(d_ff // tf):
        sl = slice(c * tf, (c + 1) * tf)
        gate = jnp.dot(u, wg_ref[:, sl], preferred_element_type=F32)
        up = jnp.dot(u, wu_ref[:, sl], preferred_element_type=F32)
        a_ref[:, sl] = (gate * jax.nn.sigmoid(gate) * up).astype(BF16)
    y = jnp.dot(a_ref[...], wd_ref[...], preferred_element_type=F32)
    res = x + 0.5 * y
    if with_final:
        res = _rmsnorm(res, fg_ref[...])
    out_ref[...] = res


def _ffn(h2d, layer, g, wg, wu, wd, *, o2d=None, wo=None, final_g=None, tm=1024, tf=256):
    t, d = h2d.shape
    d_ff = wg.shape[2]
    tm = min(tm, t)
    row = lambda i: (i, 0)
    args = [h2d]
    specs = [pl.BlockSpec((tm, d), row)]
    if o2d is not None:
        args += [o2d, wo]
        specs += [pl.BlockSpec((tm, o2d.shape[1]), row), _const_spec(wo.shape)]
    args += [g, wg, wu, wd]
    specs += [_layer_spec(a.shape, layer) for a in (g, wg, wu, wd)]
    if final_g is not None:
        args.append(final_g.reshape(1, d))
        specs.append(_const_spec((1, d)))
    kern = functools.partial(_ffn_kernel, tf=tf, with_oproj=o2d is not None,
                             with_final=final_g is not None)
    return pl.pallas_call(
        kern,
        grid=(t // tm,),
        in_specs=specs,
        out_specs=pl.BlockSpec((tm, d), row),
        out_shape=jax.ShapeDtypeStruct((t, d), F32),
        scratch_shapes=[pltpu.VMEM((tm, d_ff), BF16)],
        compiler_params=_params(("arbitrary",)),
        name="ffn",
    )(*args)


def _pool_kernel(h_ref, halo_ref, g_ref, w_ref, sc_ref, out_ref, ext_ref, *, ts):
    i = pl.program_id(1)
    g = g_ref[...]
    x = h_ref[0]
    u = _rmsnorm(x, g)
    halo = _rmsnorm(halo_ref[0], g)
    ext_ref[0:MAX_WINDOW, :] = jnp.where(i > 0, halo, 0.0)
    ext_ref[MAX_WINDOW:, :] = u
    pos = i * ts + lax.broadcasted_iota(jnp.int32, (ts, 1), 0)
    dg = w_ref.shape[1]
    for gi, w in enumerate(POOL_WINDOWS):
        cs = slice(gi * dg, (gi + 1) * dg)
        win = ext_ref[:, cs]
        shift = 1
        while shift < w:
            win = win + pltpu.roll(win, shift, 0)
            shift *= 2
        win = win[MAX_WINDOW:]
        ug = u[:, cs]
        inv_count = 1.0 / jnp.minimum(pos + 1, w).astype(F32)
        y = (win * inv_count - ug).astype(BF16)
        z = jnp.dot(y, w_ref[gi], preferred_element_type=F32)
        out_ref[0, :, cs] = x[:, cs] + z * sc_ref[:, cs]


def _pool(h, g, w_groups, scale, *, ts=512):
    b, s, d = h.shape
    ts = min(ts, s)
    halo_blocks = ts // MAX_WINDOW
    return pl.pallas_call(
        functools.partial(_pool_kernel, ts=ts),
        grid=(b, s // ts),
        in_specs=[
            pl.BlockSpec((1, ts, d), lambda bi, i: (bi, i, 0)),
            pl.BlockSpec((1, MAX_WINDOW, d),
                         lambda bi, i: (bi, jnp.maximum(i * halo_blocks - 1, 0), 0)),
            _const_spec((1, d)),
            _const_spec(w_groups.shape),
            _const_spec((1, d)),
        ],
        out_specs=pl.BlockSpec((1, ts, d), lambda bi, i: (bi, i, 0)),
        out_shape=jax.ShapeDtypeStruct((b, s, d), F32),
        scratch_shapes=[pltpu.VMEM((ts + MAX_WINDOW, d), F32)],
        compiler_params=_params(("arbitrary", "arbitrary")),
        name="pool",
    )(h, h, g.reshape(1, d), w_groups, scale.reshape(1, d))


def _kv_kernel(h_ref, g_ref, w_ref, cg_ref, cos_ref, sin_ref, k_ref, vt_ref):
    u = _rmsnorm(h_ref[0], g_ref[...]).astype(BF16)
    kv = jnp.dot(u, w_ref[...], preferred_element_type=F32)
    c = _rmsnorm(kv[:, 0:LANES], cg_ref[...])
    rope = kv[:, LANES:2 * LANES] * cos_ref[...] + kv[:, 2 * LANES:] * sin_ref[...]
    k_ref[0, 0, :, 0:LANES] = c.astype(BF16)
    k_ref[0, 0, :, LANES:] = rope.astype(BF16)
    vt_ref[0, 0, 0:KV_LORA_RANK, :] = c.T.astype(BF16)
    extra = (BF16_SUBLANES, c.shape[0])
    ones_row = lax.broadcasted_iota(jnp.int32, extra, 0) == 0
    vt_ref[0, 0, KV_LORA_RANK:, :] = jnp.where(ones_row, 1.0, 0.0).astype(BF16)


def _kv(h, g, w_kv3, ckv_g, cos_k, sin_k, *, tk):
    b, s, d = h.shape
    nkt = s // tk
    return pl.pallas_call(
        _kv_kernel,
        grid=(b, nkt),
        in_specs=[
            pl.BlockSpec((1, tk, d), lambda bi, i: (bi, i, 0)),
            _const_spec((1, d)),
            _const_spec(w_kv3.shape),
            _const_spec((1, KV_LORA_RANK)),
            pl.BlockSpec((tk, LANES), lambda bi, i: (i, 0)),
            pl.BlockSpec((tk, LANES), lambda bi, i: (i, 0)),
        ],
        out_specs=[
            pl.BlockSpec((1, 1, tk, QK_PAD), lambda bi, i: (bi, i, 0, 0)),
            pl.BlockSpec((1, 1, VT_ROWS, tk), lambda bi, i: (bi, i, 0, 0)),
        ],
        out_shape=[
            jax.ShapeDtypeStruct((b, nkt, tk, QK_PAD), BF16),
            jax.ShapeDtypeStruct((b, nkt, VT_ROWS, tk), BF16),
        ],
        compiler_params=_params(("arbitrary", "arbitrary")),
        name="kv",
    )(h, g.reshape(1, d), w_kv3, ckv_g.reshape(1, KV_LORA_RANK), cos_k, sin_k)


def _absorb_q_kernel(uk_ref, uq_ref, out_ref):
    out_ref[0] = lax.dot_general(uk_ref[0], uq_ref[0], (((1,), (1,)), ((), ())),
                                 precision=lax.Precision.HIGHEST,
                                 preferred_element_type=F32)


def _absorb_o_kernel(uv_ref, wo_ref, out_ref):
    out_ref[0] = jnp.dot(uv_ref[0], wo_ref[0], precision=lax.Precision.HIGHEST,
                         preferred_element_type=F32).astype(out_ref.dtype)


def _absorb_q(uk_h, uq_h):
    nh, lat, _ = uk_h.shape
    rank = uq_h.shape[1]
    return pl.pallas_call(
        _absorb_q_kernel,
        grid=(nh,),
        in_specs=[pl.BlockSpec((1,) + uk_h.shape[1:], lambda h: (h, 0, 0)),
                  pl.BlockSpec((1,) + uq_h.shape[1:], lambda h: (h, 0, 0))],
        out_specs=pl.BlockSpec((1, lat, rank), lambda h: (h, 0, 0)),
        out_shape=jax.ShapeDtypeStruct((nh, lat, rank), F32),
        compiler_params=_params(("arbitrary",)),
        name="absorb_q",
    )(uk_h, uq_h)


def _absorb_o(uv_h, wo_h):
    nh, lat, _ = uv_h.shape
    n = wo_h.shape[2]
    return pl.pallas_call(
        _absorb_o_kernel,
        grid=(nh,),
        in_specs=[pl.BlockSpec((1,) + uv_h.shape[1:], lambda h: (h, 0, 0)),
                  pl.BlockSpec((1,) + wo_h.shape[1:], lambda h: (h, 0, 0))],
        out_specs=pl.BlockSpec((1, lat, n), lambda h: (h, 0, 0)),
        out_shape=jax.ShapeDtypeStruct((nh, lat, n), BF16),
        compiler_params=_params(("arbitrary",)),
        name="absorb_o",
    )(uv_h, wo_h)


def _qproj_kernel(h_ref, g_ref, wdq_ref, qg_ref, wq_ref, cos_ref, sin_ref, qt_ref):
    u = _rmsnorm(h_ref[0], g_ref[...]).astype(BF16)
    scale = QK_DIM ** -0.5 * LOG2E
    cq = (_rmsnorm(jnp.dot(u, wdq_ref[...], preferred_element_type=F32),
                   qg_ref[...]) * scale).astype(BF16)
    cos = cos_ref[...]
    sin = sin_ref[...]
    tm = cq.shape[0]
    r0 = KV_LORA_RANK
    for hd in range(wq_ref.shape[0]):
        qt = lax.dot_general(wq_ref[hd], cq, (((1,), (1,)), ((), ())),
                             preferred_element_type=F32)
        x1 = qt[r0:r0 + HALF_ROPE]
        x2 = qt[r0 + HALF_ROPE:r0 + QK_ROPE_DIM]
        qt_ref[0, hd, 0:r0, :] = qt[0:r0].astype(BF16)
        qt_ref[0, hd, r0:r0 + HALF_ROPE, :] = (x1 * cos - x2 * sin).astype(BF16)
        qt_ref[0, hd, r0 + HALF_ROPE:r0 + QK_ROPE_DIM, :] = (x1 * sin + x2 * cos).astype(BF16)
        qt_ref[0, hd, r0 + QK_ROPE_DIM:, :] = jnp.zeros((QK_PAD - r0 - QK_ROPE_DIM, tm), BF16)


def _qproj(h, g, wdq, qg, wq_t, cos_t, sin_t, *, tm=1024):
    b, s, d = h.shape
    nh = wq_t.shape[0]
    rank = wdq.shape[1]
    tm = min(tm, s)
    return pl.pallas_call(
        _qproj_kernel,
        grid=(b, s // tm),
        in_specs=[
            pl.BlockSpec((1, tm, d), lambda bi, i: (bi, i, 0)),
            _const_spec((1, d)),
            _const_spec(wdq.shape),
            _const_spec((1, rank)),
            _const_spec(wq_t.shape),
            pl.BlockSpec((HALF_ROPE, tm), lambda bi, i: (0, i)),
            pl.BlockSpec((HALF_ROPE, tm), lambda bi, i: (0, i)),
        ],
        out_specs=pl.BlockSpec((1, nh, QK_PAD, tm), lambda bi, i: (bi, 0, 0, i)),
        out_shape=jax.ShapeDtypeStruct((b, nh, QK_PAD, s), BF16),
        compiler_params=_params(("arbitrary", "arbitrary")),
        name="qproj",
    )(h, g.reshape(1, d), wdq, qg.reshape(1, rank), wq_t, cos_t, sin_t)


def _attn_kernel(qt_ref, k_ref, vt_ref, o_ref, acc_ref, m_ref, s_ref, p_ref, cm_ref, *,
                 hb, tq, tk):
    qi = pl.program_id(2)
    kd = (qi * tq) // tk
    m_ref[...] = jnp.full(m_ref.shape, MASK_VALUE, F32)
    acc_ref[...] = jnp.zeros(acc_ref.shape, F32)

    def qk_head(k, slot, j):
        s = jnp.dot(k, qt_ref[0, j], preferred_element_type=F32)
        s_ref[slot, j] = s.astype(BF16)
        cm_ref[slot, j] = jnp.max(s, axis=0, keepdims=True)

    def softmax_head(slot, j, keep):
        s = s_ref[slot, j]
        if keep is None:
            cm = cm_ref[slot, j]
        else:
            s = jnp.where(keep, s, MASK_VALUE)
            cm = jnp.max(s, axis=0, keepdims=True).astype(F32)
        m_old = m_ref[j]
        m_new = jnp.maximum(m_old, cm).astype(BF16)
        m_ref[j] = m_new.astype(F32)
        p_ref[j] = jnp.exp2(s - m_new)
        return jnp.exp2(m_old - m_new.astype(F32))

    def pv_head(vt, j, alpha):
        pv = jnp.dot(vt, p_ref[j], preferred_element_type=F32)
        acc_ref[j] = alpha * acc_ref[j] + pv

    def stage(kt_qk, slot_qk, kt_sm, slot_sm, masked=False):
        k = None if kt_qk is None else k_ref[0, kt_qk]
        vt = vt_ref[0, kt_sm]
        keep = None
        if masked:
            kpos = kt_sm * tk + lax.broadcasted_iota(jnp.int32, (tk, tq), 0)
            qpos = qi * tq + lax.broadcasted_iota(jnp.int32, (tk, tq), 1)
            keep = kpos <= qpos
        for j in range(hb):
            if k is not None:
                qk_head(k, slot_qk, j)
            pv_head(vt, j, softmax_head(slot_sm, j, keep))

    k_diag = k_ref[0, kd]
    for j in range(hb):
        qk_head(k_diag, 0, j)
    stage(jnp.maximum(kd - 1, 0), 1, kd, 0, masked=True)

    def body(i, carry):
        idx = 2 * i + 1
        stage(kd - (idx + 1), 0, kd - idx, 1)
        stage(jnp.maximum(kd - (idx + 2), 0), 1, kd - (idx + 1), 0)
        return carry

    lax.fori_loop(0, kd // 2, body, 0)

    @pl.when(lax.rem(kd, 2) == 1)
    def _():
        stage(None, None, 0, 1)

    for j in range(hb):
        o = acc_ref[j, 0:V_DIM, :] * (1.0 / acc_ref[j, V_DIM:V_DIM + 1, :])
        o_ref[0, :, j * V_DIM:(j + 1) * V_DIM] = o.T.astype(o_ref.dtype)


def _attn(qt, k, vt, *, hb=16, tq=256, kv_buffers=1):
    b, nh, _, s = qt.shape
    nkt, tk = k.shape[1], k.shape[2]
    assert tk % tq == 0 and vt.shape[2] == VT_ROWS
    return pl.pallas_call(
        functools.partial(_attn_kernel, hb=hb, tq=tq, tk=tk),
        grid=(b, nh // hb, s // tq),
        in_specs=[
            pl.BlockSpec((1, hb, QK_PAD, tq), lambda bi, hg, qi: (bi, hg, 0, qi)),
            pl.BlockSpec((1, nkt, tk, QK_PAD), lambda bi, hg, qi: (bi, 0, 0, 0),
                         pipeline_mode=pl.Buffered(kv_buffers)),
            pl.BlockSpec((1, nkt, VT_ROWS, tk), lambda bi, hg, qi: (bi, 0, 0, 0),
                         pipeline_mode=pl.Buffered(kv_buffers)),
        ],
        out_specs=pl.BlockSpec((1, tq, hb * V_DIM), lambda bi, hg, qi: (bi, qi, hg)),
        out_shape=jax.ShapeDtypeStruct((b, s, nh * V_DIM), BF16),
        scratch_shapes=[
            pltpu.VMEM((hb, VT_ROWS, tq), F32),
            pltpu.VMEM((hb, 1, tq), F32),
            pltpu.VMEM((2, hb, tk, tq), BF16),
            pltpu.VMEM((hb, tk, tq), BF16),
            pltpu.VMEM((2, hb, 1, tq), F32),
        ],
        compiler_params=_params(("arbitrary", "arbitrary", "arbitrary")),
        name="attn",
    )(qt, k, vt)


def _rope_tables(seq):
    pos = jnp.arange(seq, dtype=F32)
    inv_freq = ROPE_THETA ** (-jnp.arange(0, QK_ROPE_DIM, 2, dtype=F32) / QK_ROPE_DIM)
    ang = pos[:, None] * inv_freq[None, :]
    return jnp.cos(ang), jnp.sin(ang)


def kernel(x, ffn_pre_norm, ffn_pre_wg, ffn_pre_wu, ffn_pre_wd, mix_norm, ffn_post_norm, ffn_post_wg, ffn_post_wu, ffn_post_wd, pool_w, pool_scale, kv_in_norm, w_dkv, ckv_norm, w_uk, w_uv, q_lora_norm, w_dq, w_uq, w_o, final_norm):
    b, s, d = x.shape
    depth = ffn_pre_norm.shape[0]
    n_a = pool_w.shape[0]
    t = b * s
    tq = min(256, s)
    tk = min(512, s)

    pre = [w.astype(BF16) for w in (ffn_pre_wg, ffn_pre_wu, ffn_pre_wd)]
    post = [w.astype(BF16) for w in (ffn_post_wg, ffn_post_wu, ffn_post_wd)]
    pre_g = ffn_pre_norm.reshape(depth, 1, d)
    post_g = ffn_post_norm.reshape(depth, 1, d)
    pool_w_bf = pool_w.astype(BF16)

    cos, sin = _rope_tables(s)
    pad = jnp.zeros((s, LANES - QK_ROPE_DIM), F32)
    cos_k = jnp.concatenate([cos, cos, pad], axis=1)
    sin_k = jnp.concatenate([sin, sin, pad], axis=1)
    cos_t, sin_t = cos.T, sin.T

    wr = w_dkv[:, KV_LORA_RANK:]
    wpad = jnp.zeros((d, LANES - QK_ROPE_DIM), F32)
    w_kv3 = jnp.concatenate(
        [w_dkv[:, :KV_LORA_RANK], wr, wpad,
         -wr[:, HALF_ROPE:], wr[:, :HALF_ROPE], wpad], axis=1).astype(BF16)

    uk_h = jnp.transpose(w_uk, (1, 0, 2))
    uv_h = jnp.transpose(w_uv, (1, 0, 2))
    n_b = w_dq.shape[0]
    wq_ts, wo_abs = [], []
    for j in range(n_b):
        uq_h = jnp.transpose(w_uq[j], (1, 0, 2))
        a_nope = _absorb_q(uk_h, uq_h[:, :, :QK_NOPE_DIM])
        a_rope = jnp.transpose(uq_h[:, :, QK_NOPE_DIM:], (0, 2, 1))
        wq_ts.append(jnp.concatenate([a_nope, a_rope], axis=1).astype(BF16))
        wo_h = w_o[j].reshape(N_HEADS, V_DIM, d)
        wo_abs.append(_absorb_o(uv_h, wo_h).reshape(N_HEADS * KV_LORA_RANK, d))

    h = x.reshape(t, d)
    k_lat = vt_lat = None
    o_prev = wo_prev = None
    for l in range(depth):
        h = _ffn(h, l, pre_g, *pre)
        if l < n_a:
            h = _pool(h.reshape(b, s, d), mix_norm[l], pool_w_bf[l], pool_scale[l]).reshape(t, d)
            o_prev = wo_prev = None
        else:
            j = l - n_a
            qt = _qproj(h.reshape(b, s, d), mix_norm[l], w_dq[j].astype(BF16),
                        q_lora_norm[j], wq_ts[j], cos_t, sin_t)
            o_prev = _attn(qt, k_lat, vt_lat, tq=tq, kv_buffers=1 + j % 2).reshape(t, N_HEADS * V_DIM)
            wo_prev = wo_abs[j]
        h = _ffn(h, l, post_g, *post, o2d=o_prev, wo=wo_prev,
                 final_g=final_norm if l == depth - 1 else None)
        if l == n_a - 1:
            k_lat, vt_lat = _kv(h.reshape(b, s, d), kv_in_norm, w_kv3, ckv_norm,
                                cos_k, sin_k, tk=tk)
    return h.reshape(b, s, d)
```

```python
import functools

import jax
import jax.numpy as jnp
from jax import lax
from jax.experimental import pallas as pl
from jax.experimental.pallas import tpu as pltpu

RMS_EPS = 1e-6
POOL_WINDOWS = (2, 4, 8, 16)
MAX_WINDOW = max(POOL_WINDOWS)
N_HEADS = 16
QK_NOPE_DIM = 128
QK_ROPE_DIM = 64
QK_DIM = QK_NOPE_DIM + QK_ROPE_DIM
V_DIM = 128
KV_LORA_RANK = 128
ROPE_THETA = 10000.0
HALF_ROPE = QK_ROPE_DIM // 2

VMEM_LIMIT_BYTES = 56 * 1024 * 1024
LANES = 128
QK_PAD = 256
BF16_SUBLANES = 16
VT_ROWS = KV_LORA_RANK + BF16_SUBLANES
LOG2E = 1.4426950408889634
MASK_VALUE = -1e30

BF16 = jnp.bfloat16
F32 = jnp.float32


def _rmsnorm(x, g):
    ms = jnp.mean(x * x, axis=-1, keepdims=True)
    return x * lax.rsqrt(ms + RMS_EPS) * g


def _const_spec(shape):
    nd = len(shape)
    return pl.BlockSpec(shape, lambda *_: (0,) * nd, pipeline_mode=pl.Buffered(1))


def _layer_spec(stacked_shape, layer):
    rest = tuple(stacked_shape[1:])
    zeros = (0,) * len(rest)
    return pl.BlockSpec((None,) + rest, lambda *_: (layer,) + zeros,
                        pipeline_mode=pl.Buffered(1))


def _params(sem):
    return pltpu.CompilerParams(dimension_semantics=sem,
                                vmem_limit_bytes=VMEM_LIMIT_BYTES)


def _ffn_kernel(*refs, tf, with_oproj, with_final):
    it = iter(refs)
    h_ref = next(it)
    if with_oproj:
        o_ref = next(it)
        wo_ref = next(it)
    g_ref, wg_ref, wu_ref, wd_ref = next(it), next(it), next(it), next(it)
    if with_final:
        fg_ref = next(it)
    out_ref = next(it)
    a_ref = next(it)

    x = h_ref[...]
    if with_oproj:
        x = x + jnp.dot(o_ref[...], wo_ref[...], preferred_element_type=F32)
    u = _rmsnorm(x, g_ref[...]).astype(BF16)
    d_ff = wg_ref.shape[1]
    for c in range(d_ff // tf):
        sl = slice(c * tf, (c + 1) * tf)
        gate = jnp.dot(u, wg_ref[:, sl], preferred_element_type=F32)
        up = jnp.dot(u, wu_ref[:, sl], preferred_element_type=F32)
        a_ref[:, sl] = (gate * jax.nn.sigmoid(gate) * up).astype(BF16)
    y = jnp.dot(a_ref[...], wd_ref[...], preferred_element_type=F32)
    res = x + 0.5 * y
    if with_final:
        res = _rmsnorm(res, fg_ref[...])
    out_ref[...] = res


def _ffn(h2d, layer, g, wg, wu, wd, *, o2d=None, wo=None, final_g=None, tm=1024, tf=256):
    t, d = h2d.shape
    d_ff = wg.shape[2]
    tm = min(tm, t)
    row = lambda i: (i, 0)
    args = [h2d]
    specs = [pl.BlockSpec((tm, d), row)]
    if o2d is not None:
        args += [o2d, wo]
        specs += [pl.BlockSpec((tm, o2d.shape[1]), row), _const_spec(wo.shape)]
    args += [g, wg, wu, wd]
    specs += [_layer_spec(a.shape, layer) for a in (g, wg, wu, wd)]
    if final_g is not None:
        args.append(final_g.reshape(1, d))
        specs.append(_const_spec((1, d)))
    kern = functools.partial(_ffn_kernel, tf=tf, with_oproj=o2d is not None,
                             with_final=final_g is not None)
    return pl.pallas_call(
        kern,
        grid=(t // tm,),
        in_specs=specs,
        out_specs=pl.BlockSpec((tm, d), row),
        out_shape=jax.ShapeDtypeStruct((t, d), F32),
        scratch_shapes=[pltpu.VMEM((tm, d_ff), BF16)],
        compiler_params=_params(("arbitrary",)),
        name="ffn",
    )(*args)


def _pool_kernel(h_ref, halo_ref, g_ref, w_ref, sc_ref, out_ref, ext_ref, *, ts):
    i = pl.program_id(1)
    g = g_ref[...]
    x = h_ref[0]
    u = _rmsnorm(x, g)
    halo = _rmsnorm(halo_ref[0], g)
    ext_ref[0:MAX_WINDOW, :] = jnp.where(i > 0, halo, 0.0)
    ext_ref[MAX_WINDOW:, :] = u
    pos = i * ts + lax.broadcasted_iota(jnp.int32, (ts, 1), 0)
    dg = w_ref.shape[1]
    for gi, w in enumerate(POOL_WINDOWS):
        cs = slice(gi * dg, (gi + 1) * dg)
        win = ext_ref[:, cs]
        shift = 1
        while shift < w:
            win = win + pltpu.roll(win, shift, 0)
            shift *= 2
        win = win[MAX_WINDOW:]
        ug = u[:, cs]
        inv_count = 1.0 / jnp.minimum(pos + 1, w).astype(F32)
        y = (win * inv_count - ug).astype(BF16)
        z = jnp.dot(y, w_ref[gi], preferred_element_type=F32)
        out_ref[0, :, cs] = x[:, cs] + z * sc_ref[:, cs]


def _pool(h, g, w_groups, scale, *, ts=512):
    b, s, d = h.shape
    ts = min(ts, s)
    halo_blocks = ts // MAX_WINDOW
    return pl.pallas_call(
        functools.partial(_pool_kernel, ts=ts),
        grid=(b, s // ts),
        in_specs=[
            pl.BlockSpec((1, ts, d), lambda bi, i: (bi, i, 0)),
            pl.BlockSpec((1, MAX_WINDOW, d),
                         lambda bi, i: (bi, jnp.maximum(i * halo_blocks - 1, 0), 0)),
            _const_spec((1, d)),
            _const_spec(w_groups.shape),
            _const_spec((1, d)),
        ],
        out_specs=pl.BlockSpec((1, ts, d), lambda bi, i: (bi, i, 0)),
        out_shape=jax.ShapeDtypeStruct((b, s, d), F32),
        scratch_shapes=[pltpu.VMEM((ts + MAX_WINDOW, d), F32)],
        compiler_params=_params(("arbitrary", "arbitrary")),
        name="pool",
    )(h, h, g.reshape(1, d), w_groups, scale.reshape(1, d))


def _kv_kernel(h_ref, g_ref, w_ref, cg_ref, cos_ref, sin_ref, k_ref, vt_ref):
    u = _rmsnorm(h_ref[0], g_ref[...]).astype(BF16)
    kv = jnp.dot(u, w_ref[...], preferred_element_type=F32)
    c = _rmsnorm(kv[:, 0:LANES], cg_ref[...])
    rope = kv[:, LANES:2 * LANES] * cos_ref[...] + kv[:, 2 * LANES:] * sin_ref[...]
    k_ref[0, 0, :, 0:LANES] = c.astype(BF16)
    k_ref[0, 0, :, LANES:] = rope.astype(BF16)
    vt_ref[0, 0, 0:KV_LORA_RANK, :] = c.T.astype(BF16)
    extra = (BF16_SUBLANES, c.shape[0])
    ones_row = lax.broadcasted_iota(jnp.int32, extra, 0) == 0
    vt_ref[0, 0, KV_LORA_RANK:, :] = jnp.where(ones_row, 1.0, 0.0).astype(BF16)


def _kv(h, g, w_kv3, ckv_g, cos_k, sin_k, *, tk):
    b, s, d = h.shape
    nkt = s // tk
    return pl.pallas_call(
        _kv_kernel,
        grid=(b, nkt),
        in_specs=[
            pl.BlockSpec((1, tk, d), lambda bi, i: (bi, i, 0)),
            _const_spec((1, d)),
            _const_spec(w_kv3.shape),
            _const_spec((1, KV_LORA_RANK)),
            pl.BlockSpec((tk, LANES), lambda bi, i: (i, 0)),
            pl.BlockSpec((tk, LANES), lambda bi, i: (i, 0)),
        ],
        out_specs=[
            pl.BlockSpec((1, 1, tk, QK_PAD), lambda bi, i: (bi, i, 0, 0)),
            pl.BlockSpec((1, 1, VT_ROWS, tk), lambda bi, i: (bi, i, 0, 0)),
        ],
        out_shape=[
            jax.ShapeDtypeStruct((b, nkt, tk, QK_PAD), BF16),
            jax.ShapeDtypeStruct((b, nkt, VT_ROWS, tk), BF16),
        ],
        compiler_params=_params(("arbitrary", "arbitrary")),
        name="kv",
    )(h, g.reshape(1, d), w_kv3, ckv_g.reshape(1, KV_LORA_RANK), cos_k, sin_k)


def _absorb_q_kernel(uk_ref, uq_ref, out_ref):
    out_ref[0] = lax.dot_general(uk_ref[0], uq_ref[0], (((1,), (1,)), ((), ())),
                                 precision=lax.Precision.HIGHEST,
                                 preferred_element_type=F32)


def _absorb_o_kernel(uv_ref, wo_ref, out_ref):
    out_ref[0] = jnp.dot(uv_ref[0], wo_ref[0], precision=lax.Precision.HIGHEST,
                         preferred_element_type=F32).astype(out_ref.dtype)


def _absorb_q(uk_h, uq_h):
    nh, lat, _ = uk_h.shape
    rank = uq_h.shape[1]
    return pl.pallas_call(
        _absorb_q_kernel,
        grid=(nh,),
        in_specs=[pl.BlockSpec((1,) + uk_h.shape[1:], lambda h: (h, 0, 0)),
                  pl.BlockSpec((1,) + uq_h.shape[1:], lambda h: (h, 0, 0))],
        out_specs=pl.BlockSpec((1, lat, rank), lambda h: (h, 0, 0)),
        out_shape=jax.ShapeDtypeStruct((nh, lat, rank), F32),
        compiler_params=_params(("arbitrary",)),
        name="absorb_q",
    )(uk_h, uq_h)


def _absorb_o(uv_h, wo_h):
    nh, lat, _ = uv_h.shape
    n = wo_h.shape[2]
    return pl.pallas_call(
        _absorb_o_kernel,
        grid=(nh,),
        in_specs=[pl.BlockSpec((1,) + uv_h.shape[1:], lambda h: (h, 0, 0)),
                  pl.BlockSpec((1,) + wo_h.shape[1:], lambda h: (h, 0, 0))],
        out_specs=pl.BlockSpec((1, lat, n), lambda h: (h, 0, 0)),
        out_shape=jax.ShapeDtypeStruct((nh, lat, n), BF16),
        compiler_params=_params(("arbitrary",)),
        name="absorb_o",
    )(uv_h, wo_h)


def _qproj_kernel(h_ref, g_ref, wdq_ref, qg_ref, wq_ref, cos_ref, sin_ref, qt_ref):
    u = _rmsnorm(h_ref[0], g_ref[...]).astype(BF16)
    scale = QK_DIM ** -0.5 * LOG2E
    cq = (_rmsnorm(jnp.dot(u, wdq_ref[...], preferred_element_type=F32),
                   qg_ref[...]) * scale).astype(BF16)
    cos = cos_ref[...]
    sin = sin_ref[...]
    tm = cq.shape[0]
    r0 = KV_LORA_RANK
    for hd in range(wq_ref.shape[0]):
        qt = lax.dot_general(wq_ref[hd], cq, (((1,), (1,)), ((), ())),
                             preferred_element_type=F32)
        x1 = qt[r0:r0 + HALF_ROPE]
        x2 = qt[r0 + HALF_ROPE:r0 + QK_ROPE_DIM]
        qt_ref[0, hd, 0:r0, :] = qt[0:r0].astype(BF16)
        qt_ref[0, hd, r0:r0 + HALF_ROPE, :] = (x1 * cos - x2 * sin).astype(BF16)
        qt_ref[0, hd, r0 + HALF_ROPE:r0 + QK_ROPE_DIM, :] = (x1 * sin + x2 * cos).astype(BF16)
        qt_ref[0, hd, r0 + QK_ROPE_DIM:, :] = jnp.zeros((QK_PAD - r0 - QK_ROPE_DIM, tm), BF16)


def _qproj(h, g, wdq, qg, wq_t, cos_t, sin_t, *, tm=1024):
    b, s, d = h.shape
    nh = wq_t.shape[0]
    rank = wdq.shape[1]
    tm = min(tm, s)
    return pl.pallas_call(
        _qproj_kernel,
        grid=(b, s // tm),
        in_specs=[
            pl.BlockSpec((1, tm, d), lambda bi, i: (bi, i, 0)),
            _const_spec((1, d)),
            _const_spec(wdq.shape),
            _const_spec((1, rank)),
            _const_spec(wq_t.shape),
            pl.BlockSpec((HALF_ROPE, tm), lambda bi, i: (0, i)),
            pl.BlockSpec((HALF_ROPE, tm), lambda bi, i: (0, i)),
        ],
        out_specs=pl.BlockSpec((1, nh, QK_PAD, tm), lambda bi, i: (bi, 0, 0, i)),
        out_shape=jax.ShapeDtypeStruct((b, nh, QK_PAD, s), BF16),
        compiler_params=_params(("arbitrary", "arbitrary")),
        name="qproj",
    )(h, g.reshape(1, d), wdq, qg.reshape(1, rank), wq_t, cos_t, sin_t)


def _attn_kernel(qt_ref, k_ref, vt_ref, o_ref, acc_ref, m_ref, s_ref, p_ref, cm_ref, *,
                 hb, tq, tk, qsub):
    g = pl.program_id(2)
    kd = (g * qsub * tq) // tk
    units = [(e, j) for e in range(qsub) for j in range(hb)]
    m_ref[...] = jnp.full(m_ref.shape, MASK_VALUE, F32)
    acc_ref[...] = jnp.zeros(acc_ref.shape, F32)

    def qk_unit(k, u):
        e, j = units[u]
        s = jnp.dot(k, qt_ref[0, j, :, e * tq:(e + 1) * tq],
                    preferred_element_type=F32)
        s_ref[u] = s.astype(BF16)
        cm_ref[u] = jnp.max(s, axis=0, keepdims=True)

    def softmax_unit(u, keeps):
        s = s_ref[u]
        if keeps is None:
            cm = cm_ref[u]
        else:
            s = jnp.where(keeps[units[u][0]], s, MASK_VALUE)
            cm = jnp.max(s, axis=0, keepdims=True).astype(F32)
        m_old = m_ref[u]
        m_new = jnp.maximum(m_old, cm).astype(BF16)
        m_ref[u] = m_new.astype(F32)
        p_ref[u] = jnp.exp2(s - m_new)
        return jnp.exp2(m_old - m_new.astype(F32))

    def pv_unit(vt, u, alpha):
        pv = jnp.dot(vt, p_ref[u], preferred_element_type=F32)
        acc_ref[u] = alpha * acc_ref[u] + pv

    def stage(kt_next, kt_cur, masked=False):
        k = None if kt_next is None else k_ref[0, kt_next]
        vt = vt_ref[0, kt_cur]
        keeps = None
        if masked:
            kpos = kt_cur * tk + lax.broadcasted_iota(jnp.int32, (tk, tq), 0)
            col = lax.broadcasted_iota(jnp.int32, (tk, tq), 1)
            keeps = [kpos <= (g * qsub + e) * tq + col for e in range(qsub)]
        for u in range(len(units)):
            alpha = softmax_unit(u, keeps)
            if k is not None:
                qk_unit(k, u)
            pv_unit(vt, u, alpha)

    k_diag = k_ref[0, kd]
    for u in range(len(units)):
        qk_unit(k_diag, u)
    stage(jnp.maximum(kd - 1, 0), kd, masked=True)

    def body(i, carry):
        idx = 2 * i + 1
        stage(kd - (idx + 1), kd - idx)
        stage(jnp.maximum(kd - (idx + 2), 0), kd - (idx + 1))
        return carry

    lax.fori_loop(0, kd // 2, body, 0)

    @pl.when(lax.rem(kd, 2) == 1)
    def _():
        stage(None, 0)

    for u, (e, j) in enumerate(units):
        o = acc_ref[u, 0:V_DIM, :] * (1.0 / acc_ref[u, V_DIM:V_DIM + 1, :])
        o_ref[0, e * tq:(e + 1) * tq, j * V_DIM:(j + 1) * V_DIM] = o.T.astype(o_ref.dtype)


def _attn(qt, k, vt, *, hb=16, tq=256):
    b, nh, _, s = qt.shape
    nkt, tk = k.shape[1], k.shape[2]
    qsub = tk // tq
    assert tk == qsub * tq and vt.shape[2] == VT_ROWS
    n_units = hb * qsub
    return pl.pallas_call(
        functools.partial(_attn_kernel, hb=hb, tq=tq, tk=tk, qsub=qsub),
        grid=(b, nh // hb, s // tk),
        in_specs=[
            pl.BlockSpec((1, hb, QK_PAD, tk), lambda bi, hg, gi: (bi, hg, 0, gi)),
            pl.BlockSpec((1, nkt, tk, QK_PAD), lambda bi, hg, gi: (bi, 0, 0, 0),
                         pipeline_mode=pl.Buffered(1)),
            pl.BlockSpec((1, nkt, VT_ROWS, tk), lambda bi, hg, gi: (bi, 0, 0, 0),
                         pipeline_mode=pl.Buffered(1)),
        ],
        out_specs=pl.BlockSpec((1, tk, hb * V_DIM), lambda bi, hg, gi: (bi, gi, hg)),
        out_shape=jax.ShapeDtypeStruct((b, s, nh * V_DIM), BF16),
        scratch_shapes=[
            pltpu.VMEM((n_units, VT_ROWS, tq), F32),
            pltpu.VMEM((n_units, 1, tq), F32),
            pltpu.VMEM((n_units, tk, tq), BF16),
            pltpu.VMEM((n_units, tk, tq), BF16),
            pltpu.VMEM((n_units, 1, tq), F32),
        ],
        compiler_params=_params(("arbitrary", "arbitrary", "arbitrary")),
        name="attn",
    )(qt, k, vt)


def _rope_tables(seq):
    pos = jnp.arange(seq, dtype=F32)
    inv_freq = ROPE_THETA ** (-jnp.arange(0, QK_ROPE_DIM, 2, dtype=F32) / QK_ROPE_DIM)
    ang = pos[:, None] * inv_freq[None, :]
    return jnp.cos(ang), jnp.sin(ang)


def kernel(x, ffn_pre_norm, ffn_pre_wg, ffn_pre_wu, ffn_pre_wd, mix_norm, ffn_post_norm, ffn_post_wg, ffn_post_wu, ffn_post_wd, pool_w, pool_scale, kv_in_norm, w_dkv, ckv_norm, w_uk, w_uv, q_lora_norm, w_dq, w_uq, w_o, final_norm):
    b, s, d = x.shape
    depth = ffn_pre_norm.shape[0]
    n_a = pool_w.shape[0]
    t = b * s
    tq = min(256, s)
    tk = min(512, s)

    pre = [w.astype(BF16) for w in (ffn_pre_wg, ffn_pre_wu, ffn_pre_wd)]
    post = [w.astype(BF16) for w in (ffn_post_wg, ffn_post_wu, ffn_post_wd)]
    pre_g = ffn_pre_norm.reshape(depth, 1, d)
    post_g = ffn_post_norm.reshape(depth, 1, d)
    pool_w_bf = pool_w.astype(BF16)

    cos, sin = _rope_tables(s)
    pad = jnp.zeros((s, LANES - QK_ROPE_DIM), F32)
    cos_k = jnp.concatenate([cos, cos, pad], axis=1)
    sin_k = jnp.concatenate([sin, sin, pad], axis=1)
    cos_t, sin_t = cos.T, sin.T

    wr = w_dkv[:, KV_LORA_RANK:]
    wpad = jnp.zeros((d, LANES - QK_ROPE_DIM), F32)
    w_kv3 = jnp.concatenate(
        [w_dkv[:, :KV_LORA_RANK], wr, wpad,
         -wr[:, HALF_ROPE:], wr[:, :HALF_ROPE], wpad], axis=1).astype(BF16)

    uk_h = jnp.transpose(w_uk, (1, 0, 2))
    uv_h = jnp.transpose(w_uv, (1, 0, 2))
    n_b = w_dq.shape[0]
    wq_ts, wo_abs = [], []
    for j in range(n_b):
        uq_h = jnp.transpose(w_uq[j], (1, 0, 2))
        a_nope = _absorb_q(uk_h, uq_h[:, :, :QK_NOPE_DIM])
        a_rope = jnp.transpose(uq_h[:, :, QK_NOPE_DIM:], (0, 2, 1))
        wq_ts.append(jnp.concatenate([a_nope, a_rope], axis=1).astype(BF16))
        wo_h = w_o[j].reshape(N_HEADS, V_DIM, d)
        wo_abs.append(_absorb_o(uv_h, wo_h).reshape(N_HEADS * KV_LORA_RANK, d))

    h = x.reshape(t, d)
    k_lat = vt_lat = None
    o_prev = wo_prev = None
    for l in range(depth):
        h = _ffn(h, l, pre_g, *pre)
        if l < n_a:
            h = _pool(h.reshape(b, s, d), mix_norm[l], pool_w_bf[l], pool_scale[l]).reshape(t, d)
            o_prev = wo_prev = None
        else:
            j = l - n_a
            qt = _qproj(h.reshape(b, s, d), mix_norm[l], w_dq[j].astype(BF16),
                        q_lora_norm[j], wq_ts[j], cos_t, sin_t)
            o_prev = _attn(qt, k_lat, vt_lat, tq=tq).reshape(t, N_HEADS * V_DIM)
            wo_prev = wo_abs[j]
        h = _ffn(h, l, post_g, *post, o2d=o_prev, wo=wo_prev,
                 final_g=final_norm if l == depth - 1 else None)
        if l == n_a - 1:
            k_lat, vt_lat = _kv(h.reshape(b, s, d), kv_in_norm, w_kv3, ckv_norm,
                                cos_k, sin_k, tk=tk)
    return h.reshape(b, s, d)
```

```python
import functools

import jax
import jax.numpy as jnp
from jax import lax
from jax.experimental import pallas as pl
from jax.experimental.pallas import tpu as pltpu

RMS_EPS = 1e-6
POOL_WINDOWS = (2, 4, 8, 16)
MAX_WINDOW = max(POOL_WINDOWS)
N_HEADS = 16
QK_NOPE_DIM = 128
QK_ROPE_DIM = 64
QK_DIM = QK_NOPE_DIM + QK_ROPE_DIM
V_DIM = 128
KV_LORA_RANK = 128
ROPE_THETA = 10000.0
HALF_ROPE = QK_ROPE_DIM // 2

VMEM_LIMIT_BYTES = 56 * 1024 * 1024
LANES = 128
QK_PAD = 256
BF16_SUBLANES = 16
VT_ROWS = KV_LORA_RANK + BF16_SUBLANES
LOG2E = 1.4426950408889634
MASK_VALUE = -1e30

FFN_TOKEN_TILE = 1024
FFN_HIDDEN_CHUNK = 256
POOL_TOKEN_TILE = 512
QPROJ_TOKEN_TILE = 1024
ATTN_QUERY_TILE = 256
ATTN_KEY_TILE = 512
ATTN_HEADS_PER_STEP = 16

BF16 = jnp.bfloat16
F32 = jnp.float32


def _rmsnorm(x, g):
    ms = jnp.mean(x * x, axis=-1, keepdims=True)
    return x * lax.rsqrt(ms + RMS_EPS) * g


def _const_spec(shape):
    nd = len(shape)
    return pl.BlockSpec(shape, lambda *_: (0,) * nd, pipeline_mode=pl.Buffered(1))


def _layer_spec(stacked_shape, layer):
    rest = tuple(stacked_shape[1:])
    zeros = (0,) * len(rest)
    return pl.BlockSpec((None,) + rest, lambda *_: (layer,) + zeros,
                        pipeline_mode=pl.Buffered(1))


def _params(sem):
    return pltpu.CompilerParams(dimension_semantics=sem,
                                vmem_limit_bytes=VMEM_LIMIT_BYTES)


def _ffn_kernel(*refs, tf, with_oproj, with_final):
    it = iter(refs)
    h_ref = next(it)
    if with_oproj:
        o_ref = next(it)
        wo_ref = next(it)
    g_ref, wg_ref, wu_ref, wd_ref = next(it), next(it), next(it), next(it)
    if with_final:
        fg_ref = next(it)
    out_ref = next(it)
    a_ref = next(it)

    x = h_ref[...]
    if with_oproj:
        x = x + jnp.dot(o_ref[...], wo_ref[...], preferred_element_type=F32)
    u = _rmsnorm(x, g_ref[...]).astype(BF16)
    d_ff = wg_ref.shape[1]
    for c in range(d_ff // tf):
        sl = slice(c * tf, (c + 1) * tf)
        gate = jnp.dot(u, wg_ref[:, sl], preferred_element_type=F32)
        up = jnp.dot(u, wu_ref[:, sl], preferred_element_type=F32)
        a_ref[:, sl] = (gate * jax.nn.sigmoid(gate) * up).astype(BF16)
    y = jnp.dot(a_ref[...], wd_ref[...], preferred_element_type=F32)
    res = x + 0.5 * y
    if with_final:
        res = _rmsnorm(res, fg_ref[...])
    out_ref[...] = res


def _ffn(h2d, layer, g, wg, wu, wd, *, o2d=None, wo=None, final_g=None, tm=FFN_TOKEN_TILE, tf=FFN_HIDDEN_CHUNK):
    t, d = h2d.shape
    d_ff = wg.shape[2]
    tm = min(tm, t)
    row = lambda i: (i, 0)
    args = [h2d]
    specs = [pl.BlockSpec((tm, d), row)]
    if o2d is not None:
        args += [o2d, wo]
        specs += [pl.BlockSpec((tm, o2d.shape[1]), row), _const_spec(wo.shape)]
    args += [g, wg, wu, wd]
    specs += [_layer_spec(a.shape, layer) for a in (g, wg, wu, wd)]
    if final_g is not None:
        args.append(final_g.reshape(1, d))
        specs.append(_const_spec((1, d)))
    kern = functools.partial(_ffn_kernel, tf=tf, with_oproj=o2d is not None,
                             with_final=final_g is not None)
    return pl.pallas_call(
        kern,
        grid=(t // tm,),
        in_specs=specs,
        out_specs=pl.BlockSpec((tm, d), row),
        out_shape=jax.ShapeDtypeStruct((t, d), F32),
        scratch_shapes=[pltpu.VMEM((tm, d_ff), BF16)],
        compiler_params=_params(("arbitrary",)),
        name="ffn",
    )(*args)


def _pool_kernel(h_ref, halo_ref, g_ref, w_ref, sc_ref, out_ref, ext_ref, *, ts):
    i = pl.program_id(1)
    g = g_ref[...]
    x = h_ref[0]
    u = _rmsnorm(x, g)
    halo = _rmsnorm(halo_ref[0], g)
    ext_ref[0:MAX_WINDOW, :] = jnp.where(i > 0, halo, 0.0)
    ext_ref[MAX_WINDOW:, :] = u
    pos = i * ts + lax.broadcasted_iota(jnp.int32, (ts, 1), 0)
    dg = w_ref.shape[1]
    for gi, w in enumerate(POOL_WINDOWS):
        cs = slice(gi * dg, (gi + 1) * dg)
        win = ext_ref[:, cs]
        shift = 1
        while shift < w:
            win = win + pltpu.roll(win, shift, 0)
            shift *= 2
        win = win[MAX_WINDOW:]
        ug = u[:, cs]
        inv_count = 1.0 / jnp.minimum(pos + 1, w).astype(F32)
        y = (win * inv_count - ug).astype(BF16)
        z = jnp.dot(y, w_ref[gi], preferred_element_type=F32)
        out_ref[0, :, cs] = x[:, cs] + z * sc_ref[:, cs]


def _pool(h, g, w_groups, scale, *, ts=POOL_TOKEN_TILE):
    b, s, d = h.shape
    ts = min(ts, s)
    halo_blocks = ts // MAX_WINDOW
    return pl.pallas_call(
        functools.partial(_pool_kernel, ts=ts),
        grid=(b, s // ts),
        in_specs=[
            pl.BlockSpec((1, ts, d), lambda bi, i: (bi, i, 0)),
            pl.BlockSpec((1, MAX_WINDOW, d),
                         lambda bi, i: (bi, jnp.maximum(i * halo_blocks - 1, 0), 0)),
            _const_spec((1, d)),
            _const_spec(w_groups.shape),
            _const_spec((1, d)),
        ],
        out_specs=pl.BlockSpec((1, ts, d), lambda bi, i: (bi, i, 0)),
        out_shape=jax.ShapeDtypeStruct((b, s, d), F32),
        scratch_shapes=[pltpu.VMEM((ts + MAX_WINDOW, d), F32)],
        compiler_params=_params(("arbitrary", "arbitrary")),
        name="pool",
    )(h, h, g.reshape(1, d), w_groups, scale.reshape(1, d))


def _kv_kernel(h_ref, g_ref, w_ref, cg_ref, cos_ref, sin_ref, k_ref, vt_ref):
    u = _rmsnorm(h_ref[0], g_ref[...]).astype(BF16)
    kv = jnp.dot(u, w_ref[...], preferred_element_type=F32)
    c = _rmsnorm(kv[:, 0:LANES], cg_ref[...])
    rope = kv[:, LANES:2 * LANES] * cos_ref[...] + kv[:, 2 * LANES:] * sin_ref[...]
    k_ref[0, 0, :, 0:LANES] = c.astype(BF16)
    k_ref[0, 0, :, LANES:] = rope.astype(BF16)
    vt_ref[0, 0, 0:KV_LORA_RANK, :] = c.T.astype(BF16)
    extra = (BF16_SUBLANES, c.shape[0])
    ones_row = lax.broadcasted_iota(jnp.int32, extra, 0) == 0
    vt_ref[0, 0, KV_LORA_RANK:, :] = jnp.where(ones_row, 1.0, 0.0).astype(BF16)


def _kv(h, g, w_kv3, ckv_g, cos_k, sin_k, *, tk):
    b, s, d = h.shape
    nkt = s // tk
    return pl.pallas_call(
        _kv_kernel,
        grid=(b, nkt),
        in_specs=[
            pl.BlockSpec((1, tk, d), lambda bi, i: (bi, i, 0)),
            _const_spec((1, d)),
            _const_spec(w_kv3.shape),
            _const_spec((1, KV_LORA_RANK)),
            pl.BlockSpec((tk, LANES), lambda bi, i: (i, 0)),
            pl.BlockSpec((tk, LANES), lambda bi, i: (i, 0)),
        ],
        out_specs=[
            pl.BlockSpec((1, 1, tk, QK_PAD), lambda bi, i: (bi, i, 0, 0)),
            pl.BlockSpec((1, 1, VT_ROWS, tk), lambda bi, i: (bi, i, 0, 0)),
        ],
        out_shape=[
            jax.ShapeDtypeStruct((b, nkt, tk, QK_PAD), BF16),
            jax.ShapeDtypeStruct((b, nkt, VT_ROWS, tk), BF16),
        ],
        compiler_params=_params(("arbitrary", "arbitrary")),
        name="kv",
    )(h, g.reshape(1, d), w_kv3, ckv_g.reshape(1, KV_LORA_RANK), cos_k, sin_k)


def _absorb_q_kernel(uk_ref, uq_ref, out_ref):
    out_ref[0] = lax.dot_general(uk_ref[0], uq_ref[0], (((1,), (1,)), ((), ())),
                                 precision=lax.Precision.HIGHEST,
                                 preferred_element_type=F32)


def _absorb_o_kernel(uv_ref, wo_ref, out_ref):
    out_ref[0] = jnp.dot(uv_ref[0], wo_ref[0], precision=lax.Precision.HIGHEST,
                         preferred_element_type=F32).astype(out_ref.dtype)


def _absorb_q(uk_h, uq_h):
    nh, lat, _ = uk_h.shape
    rank = uq_h.shape[1]
    return pl.pallas_call(
        _absorb_q_kernel,
        grid=(nh,),
        in_specs=[pl.BlockSpec((1,) + uk_h.shape[1:], lambda h: (h, 0, 0)),
                  pl.BlockSpec((1,) + uq_h.shape[1:], lambda h: (h, 0, 0))],
        out_specs=pl.BlockSpec((1, lat, rank), lambda h: (h, 0, 0)),
        out_shape=jax.ShapeDtypeStruct((nh, lat, rank), F32),
        compiler_params=_params(("arbitrary",)),
        name="absorb_q",
    )(uk_h, uq_h)


def _absorb_o(uv_h, wo_h):
    nh, lat, _ = uv_h.shape
    n = wo_h.shape[2]
    return pl.pallas_call(
        _absorb_o_kernel,
        grid=(nh,),
        in_specs=[pl.BlockSpec((1,) + uv_h.shape[1:], lambda h: (h, 0, 0)),
                  pl.BlockSpec((1,) + wo_h.shape[1:], lambda h: (h, 0, 0))],
        out_specs=pl.BlockSpec((1, lat, n), lambda h: (h, 0, 0)),
        out_shape=jax.ShapeDtypeStruct((nh, lat, n), BF16),
        compiler_params=_params(("arbitrary",)),
        name="absorb_o",
    )(uv_h, wo_h)


def _qproj_kernel(h_ref, g_ref, wdq_ref, qg_ref, wq_ref, cos_ref, sin_ref, qt_ref):
    u = _rmsnorm(h_ref[0], g_ref[...]).astype(BF16)
    scale = QK_DIM ** -0.5 * LOG2E
    cq = (_rmsnorm(jnp.dot(u, wdq_ref[...], preferred_element_type=F32),
                   qg_ref[...]) * scale).astype(BF16)
    cos = cos_ref[...]
    sin = sin_ref[...]
    tm = cq.shape[0]
    r0 = KV_LORA_RANK
    for hd in range(wq_ref.shape[0]):
        qt = lax.dot_general(wq_ref[hd], cq, (((1,), (1,)), ((), ())),
                             preferred_element_type=F32)
        x1 = qt[r0:r0 + HALF_ROPE]
        x2 = qt[r0 + HALF_ROPE:r0 + QK_ROPE_DIM]
        qt_ref[0, hd, 0:r0, :] = qt[0:r0].astype(BF16)
        qt_ref[0, hd, r0:r0 + HALF_ROPE, :] = (x1 * cos - x2 * sin).astype(BF16)
        qt_ref[0, hd, r0 + HALF_ROPE:r0 + QK_ROPE_DIM, :] = (x1 * sin + x2 * cos).astype(BF16)
        qt_ref[0, hd, r0 + QK_ROPE_DIM:, :] = jnp.zeros((QK_PAD - r0 - QK_ROPE_DIM, tm), BF16)


def _qproj(h, g, wdq, qg, wq_t, cos_t, sin_t, *, tm=QPROJ_TOKEN_TILE):
    b, s, d = h.shape
    nh = wq_t.shape[0]
    rank = wdq.shape[1]
    tm = min(tm, s)
    return pl.pallas_call(
        _qproj_kernel,
        grid=(b, s // tm),
        in_specs=[
            pl.BlockSpec((1, tm, d), lambda bi, i: (bi, i, 0)),
            _const_spec((1, d)),
            _const_spec(wdq.shape),
            _const_spec((1, rank)),
            _const_spec(wq_t.shape),
            pl.BlockSpec((HALF_ROPE, tm), lambda bi, i: (0, i)),
            pl.BlockSpec((HALF_ROPE, tm), lambda bi, i: (0, i)),
        ],
        out_specs=pl.BlockSpec((1, nh, QK_PAD, tm), lambda bi, i: (bi, 0, 0, i)),
        out_shape=jax.ShapeDtypeStruct((b, nh, QK_PAD, s), BF16),
        compiler_params=_params(("arbitrary", "arbitrary")),
        name="qproj",
    )(h, g.reshape(1, d), wdq, qg.reshape(1, rank), wq_t, cos_t, sin_t)


def _attn_kernel(qt_ref, k_ref, vt_ref, o_ref, acc_ref, m_ref, s_ref, p_ref, cm_ref, *,
                 hb, tq, tk, qsub):
    g = pl.program_id(2)
    kd = (g * qsub * tq) // tk
    units = [(e, j) for e in range(qsub) for j in range(hb)]
    m_ref[...] = jnp.full(m_ref.shape, MASK_VALUE, F32)
    acc_ref[...] = jnp.zeros(acc_ref.shape, F32)

    def qk_unit(k, u, rows=tk):
        e, j = units[u]
        s = jnp.dot(k[0:rows], qt_ref[0, j, :, e * tq:(e + 1) * tq],
                    preferred_element_type=F32)
        s_ref[u, 0:rows] = s.astype(BF16)
        cm_ref[u] = jnp.max(s, axis=0, keepdims=True)

    def softmax_unit(u, keeps, rows=tk):
        s = s_ref[u, 0:rows]
        if keeps is None:
            cm = cm_ref[u]
        else:
            s = jnp.where(keeps[units[u][0]][0:rows], s, MASK_VALUE)
            cm = jnp.max(s, axis=0, keepdims=True).astype(F32)
        m_old = m_ref[u]
        m_new = jnp.maximum(m_old, cm).astype(BF16)
        m_ref[u] = m_new.astype(F32)
        p_ref[u, 0:rows] = jnp.exp2(s - m_new)
        return jnp.exp2(m_old - m_new.astype(F32))

    def pv_unit(vt, u, alpha, rows=tk):
        pv = jnp.dot(vt[:, 0:rows], p_ref[u, 0:rows], preferred_element_type=F32)
        acc_ref[u] = alpha * acc_ref[u] + pv

    def diag_rows(u):
        return (units[u][0] + 1) * tq

    def stage(kt_next, kt_cur, masked=False):
        k = None if kt_next is None else k_ref[0, kt_next]
        vt = vt_ref[0, kt_cur]
        keeps = None
        if masked:
            kpos = kt_cur * tk + lax.broadcasted_iota(jnp.int32, (tk, tq), 0)
            col = lax.broadcasted_iota(jnp.int32, (tk, tq), 1)
            keeps = [kpos <= (g * qsub + e) * tq + col for e in range(qsub)]
        for u in range(len(units)):
            rows = diag_rows(u) if masked else tk
            alpha = softmax_unit(u, keeps, rows)
            if k is not None:
                qk_unit(k, u)
            pv_unit(vt, u, alpha, rows)

    k_diag = k_ref[0, kd]
    for u in range(len(units)):
        qk_unit(k_diag, u, diag_rows(u))
    stage(jnp.maximum(kd - 1, 0), kd, masked=True)

    def body(i, carry):
        idx = 2 * i + 1
        stage(kd - (idx + 1), kd - idx)
        stage(jnp.maximum(kd - (idx + 2), 0), kd - (idx + 1))
        return carry

    lax.fori_loop(0, kd // 2, body, 0)

    @pl.when(lax.rem(kd, 2) == 1)
    def _():
        stage(None, 0)

    for u, (e, j) in enumerate(units):
        o = acc_ref[u, 0:V_DIM, :] * (1.0 / acc_ref[u, V_DIM:V_DIM + 1, :])
        o_ref[0, e * tq:(e + 1) * tq, j * V_DIM:(j + 1) * V_DIM] = o.T.astype(o_ref.dtype)


def _attn(qt, k, vt, *, hb=ATTN_HEADS_PER_STEP, tq=ATTN_QUERY_TILE):
    b, nh, _, s = qt.shape
    nkt, tk = k.shape[1], k.shape[2]
    qsub = tk // tq
    assert tk == qsub * tq and vt.shape[2] == VT_ROWS
    n_units = hb * qsub
    return pl.pallas_call(
        functools.partial(_attn_kernel, hb=hb, tq=tq, tk=tk, qsub=qsub),
        grid=(b, nh // hb, s // tk),
        in_specs=[
            pl.BlockSpec((1, hb, QK_PAD, tk), lambda bi, hg, gi: (bi, hg, 0, gi)),
            pl.BlockSpec((1, nkt, tk, QK_PAD), lambda bi, hg, gi: (bi, 0, 0, 0),
                         pipeline_mode=pl.Buffered(1)),
            pl.BlockSpec((1, nkt, VT_ROWS, tk), lambda bi, hg, gi: (bi, 0, 0, 0),
                         pipeline_mode=pl.Buffered(1)),
        ],
        out_specs=pl.BlockSpec((1, tk, hb * V_DIM), lambda bi, hg, gi: (bi, gi, hg)),
        out_shape=jax.ShapeDtypeStruct((b, s, nh * V_DIM), BF16),
        scratch_shapes=[
            pltpu.VMEM((n_units, VT_ROWS, tq), F32),
            pltpu.VMEM((n_units, 1, tq), F32),
            pltpu.VMEM((n_units, tk, tq), BF16),
            pltpu.VMEM((n_units, tk, tq), BF16),
            pltpu.VMEM((n_units, 1, tq), F32),
        ],
        compiler_params=_params(("arbitrary", "arbitrary", "arbitrary")),
        name="attn",
    )(qt, k, vt)


def _rope_tables(seq):
    pos = jnp.arange(seq, dtype=F32)
    inv_freq = ROPE_THETA ** (-jnp.arange(0, QK_ROPE_DIM, 2, dtype=F32) / QK_ROPE_DIM)
    ang = pos[:, None] * inv_freq[None, :]
    return jnp.cos(ang), jnp.sin(ang)


def kernel(x, ffn_pre_norm, ffn_pre_wg, ffn_pre_wu, ffn_pre_wd, mix_norm, ffn_post_norm, ffn_post_wg, ffn_post_wu, ffn_post_wd, pool_w, pool_scale, kv_in_norm, w_dkv, ckv_norm, w_uk, w_uv, q_lora_norm, w_dq, w_uq, w_o, final_norm):
    b, s, d = x.shape
    depth = ffn_pre_norm.shape[0]
    n_a = pool_w.shape[0]
    t = b * s
    tq = min(ATTN_QUERY_TILE, s)
    tk = min(ATTN_KEY_TILE, s)

    pre = [w.astype(BF16) for w in (ffn_pre_wg, ffn_pre_wu, ffn_pre_wd)]
    post = [w.astype(BF16) for w in (ffn_post_wg, ffn_post_wu, ffn_post_wd)]
    pre_g = ffn_pre_norm.reshape(depth, 1, d)
    post_g = ffn_post_norm.reshape(depth, 1, d)
    pool_w_bf = pool_w.astype(BF16)

    cos, sin = _rope_tables(s)
    pad = jnp.zeros((s, LANES - QK_ROPE_DIM), F32)
    cos_k = jnp.concatenate([cos, cos, pad], axis=1)
    sin_k = jnp.concatenate([sin, sin, pad], axis=1)
    cos_t, sin_t = cos.T, sin.T

    wr = w_dkv[:, KV_LORA_RANK:]
    wpad = jnp.zeros((d, LANES - QK_ROPE_DIM), F32)
    w_kv3 = jnp.concatenate(
        [w_dkv[:, :KV_LORA_RANK], wr, wpad,
         -wr[:, HALF_ROPE:], wr[:, :HALF_ROPE], wpad], axis=1).astype(BF16)

    uk_h = jnp.transpose(w_uk, (1, 0, 2))
    uv_h = jnp.transpose(w_uv, (1, 0, 2))
    n_b = w_dq.shape[0]
    wq_ts, wo_abs = [], []
    for j in range(n_b):
        uq_h = jnp.transpose(w_uq[j], (1, 0, 2))
        a_nope = _absorb_q(uk_h, uq_h[:, :, :QK_NOPE_DIM])
        a_rope = jnp.transpose(uq_h[:, :, QK_NOPE_DIM:], (0, 2, 1))
        wq_ts.append(jnp.concatenate([a_nope, a_rope], axis=1).astype(BF16))
        wo_h = w_o[j].reshape(N_HEADS, V_DIM, d)
        wo_abs.append(_absorb_o(uv_h, wo_h).reshape(N_HEADS * KV_LORA_RANK, d))

    h = x.reshape(t, d)
    k_lat = vt_lat = None
    o_prev = wo_prev = None
    for l in range(depth):
        h = _ffn(h, l, pre_g, *pre)
        if l < n_a:
            h = _pool(h.reshape(b, s, d), mix_norm[l], pool_w_bf[l], pool_scale[l]).reshape(t, d)
            o_prev = wo_prev = None
        else:
            j = l - n_a
            qt = _qproj(h.reshape(b, s, d), mix_norm[l], w_dq[j].astype(BF16),
                        q_lora_norm[j], wq_ts[j], cos_t, sin_t)
            o_prev = _attn(qt, k_lat, vt_lat, tq=tq).reshape(t, N_HEADS * V_DIM)
            wo_prev = wo_abs[j]
        h = _ffn(h, l, post_g, *post, o2d=o_prev, wo=wo_prev,
                 final_g=final_norm if l == depth - 1 else None)
        if l == n_a - 1:
            k_lat, vt_lat = _kv(h.reshape(b, s, d), kv_in_norm, w_kv3, ckv_norm,
                                cos_k, sin_k, tk=tk)
    return h.reshape(b, s, d)
```

```python
import functools

import jax
import jax.numpy as jnp
from jax import lax
from jax.experimental import pallas as pl
from jax.experimental.pallas import tpu as pltpu

RMS_EPS = 1e-6
POOL_WINDOWS = (2, 4, 8, 16)
MAX_WINDOW = max(POOL_WINDOWS)
N_HEADS = 16
QK_NOPE_DIM = 128
QK_ROPE_DIM = 64
QK_DIM = QK_NOPE_DIM + QK_ROPE_DIM
V_DIM = 128
KV_LORA_RANK = 128
ROPE_THETA = 10000.0
HALF_ROPE = QK_ROPE_DIM // 2

VMEM_LIMIT_BYTES = 56 * 1024 * 1024
LANES = 128
QK_PAD = 256
BF16_SUBLANES = 16
VT_ROWS = KV_LORA_RANK + BF16_SUBLANES
LOG2E = 1.4426950408889634
MASK_VALUE = -1e30

FFN_TOKEN_TILE = 1024
FFN_HIDDEN_CHUNK = 256
POOL_TOKEN_TILE = 512
QPROJ_TOKEN_TILE = 1024
ATTN_QUERY_TILE = 256
ATTN_KEY_TILE = 512
ATTN_HEADS_PER_STEP = 16

BF16 = jnp.bfloat16
F32 = jnp.float32


def _rmsnorm(x, g):
    ms = jnp.mean(x * x, axis=-1, keepdims=True)
    return x * lax.rsqrt(ms + RMS_EPS) * g


def _const_spec(shape):
    nd = len(shape)
    return pl.BlockSpec(shape, lambda *_: (0,) * nd, pipeline_mode=pl.Buffered(1))


def _layer_spec(stacked_shape, layer):
    rest = tuple(stacked_shape[1:])
    zeros = (0,) * len(rest)
    return pl.BlockSpec((None,) + rest, lambda *_: (layer,) + zeros,
                        pipeline_mode=pl.Buffered(1))


def _params(sem):
    return pltpu.CompilerParams(dimension_semantics=sem,
                                vmem_limit_bytes=VMEM_LIMIT_BYTES)


def _ffn_kernel(*refs, tf, with_oproj, with_final):
    it = iter(refs)
    h_ref = next(it)
    if with_oproj:
        o_ref = next(it)
        wo_ref = next(it)
    g_ref, wg_ref, wu_ref, wd_ref = next(it), next(it), next(it), next(it)
    if with_final:
        fg_ref = next(it)
    out_ref = next(it)
    a_ref = next(it)

    x = h_ref[...]
    if with_oproj:
        x = x + jnp.dot(o_ref[...], wo_ref[...], preferred_element_type=F32)
    u = _rmsnorm(x, g_ref[...]).astype(BF16)
    d_ff = wg_ref.shape[1]
    for c in range(d_ff // tf):
        sl = slice(c * tf, (c + 1) * tf)
        gate = jnp.dot(u, wg_ref[:, sl], preferred_element_type=F32)
        up = jnp.dot(u, wu_ref[:, sl], preferred_element_type=F32)
        a_ref[:, sl] = (gate * jax.nn.sigmoid(gate) * up).astype(BF16)
    y = jnp.dot(a_ref[...], wd_ref[...], preferred_element_type=F32)
    res = x + 0.5 * y
    if with_final:
        res = _rmsnorm(res, fg_ref[...])
    out_ref[...] = res


def _ffn(h2d, layer, g, wg, wu, wd, *, o2d=None, wo=None, final_g=None, tm=FFN_TOKEN_TILE, tf=FFN_HIDDEN_CHUNK):
    t, d = h2d.shape
    d_ff = wg.shape[2]
    tm = min(tm, t)
    row = lambda i: (i, 0)
    args = [h2d]
    specs = [pl.BlockSpec((tm, d), row)]
    if o2d is not None:
        args += [o2d, wo]
        specs += [pl.BlockSpec((tm, o2d.shape[1]), row), _const_spec(wo.shape)]
    args += [g, wg, wu, wd]
    specs += [_layer_spec(a.shape, layer) for a in (g, wg, wu, wd)]
    if final_g is not None:
        args.append(final_g.reshape(1, d))
        specs.append(_const_spec((1, d)))
    kern = functools.partial(_ffn_kernel, tf=tf, with_oproj=o2d is not None,
                             with_final=final_g is not None)
    return pl.pallas_call(
        kern,
        grid=(t // tm,),
        in_specs=specs,
        out_specs=pl.BlockSpec((tm, d), row),
        out_shape=jax.ShapeDtypeStruct((t, d), F32),
        scratch_shapes=[pltpu.VMEM((tm, d_ff), BF16)],
        compiler_params=_params(("arbitrary",)),
        name="ffn",
    )(*args)


def _pool_kernel(h_ref, halo_ref, g_ref, w_ref, sc_ref, out_ref, ext_ref, *, ts):
    i = pl.program_id(1)
    g = g_ref[...]
    x = h_ref[0]
    u = _rmsnorm(x, g)
    halo = _rmsnorm(halo_ref[0], g)
    ext_ref[0:MAX_WINDOW, :] = jnp.where(i > 0, halo, 0.0)
    ext_ref[MAX_WINDOW:, :] = u
    pos = i * ts + lax.broadcasted_iota(jnp.int32, (ts, 1), 0)
    dg = w_ref.shape[1]
    for gi, w in enumerate(POOL_WINDOWS):
        cs = slice(gi * dg, (gi + 1) * dg)
        win = ext_ref[:, cs]
        shift = 1
        while shift < w:
            win = win + pltpu.roll(win, shift, 0)
            shift *= 2
        win = win[MAX_WINDOW:]
        ug = u[:, cs]
        inv_count = 1.0 / jnp.minimum(pos + 1, w).astype(F32)
        y = (win * inv_count - ug).astype(BF16)
        z = jnp.dot(y, w_ref[gi], preferred_element_type=F32)
        out_ref[0, :, cs] = x[:, cs] + z * sc_ref[:, cs]


def _pool(h, g, w_groups, scale, *, ts=POOL_TOKEN_TILE):
    b, s, d = h.shape
    ts = min(ts, s)
    halo_blocks = ts // MAX_WINDOW
    return pl.pallas_call(
        functools.partial(_pool_kernel, ts=ts),
        grid=(b, s // ts),
        in_specs=[
            pl.BlockSpec((1, ts, d), lambda bi, i: (bi, i, 0)),
            pl.BlockSpec((1, MAX_WINDOW, d),
                         lambda bi, i: (bi, jnp.maximum(i * halo_blocks - 1, 0), 0)),
            _const_spec((1, d)),
            _const_spec(w_groups.shape),
            _const_spec((1, d)),
        ],
        out_specs=pl.BlockSpec((1, ts, d), lambda bi, i: (bi, i, 0)),
        out_shape=jax.ShapeDtypeStruct((b, s, d), F32),
        scratch_shapes=[pltpu.VMEM((ts + MAX_WINDOW, d), F32)],
        compiler_params=_params(("arbitrary", "arbitrary")),
        name="pool",
    )(h, h, g.reshape(1, d), w_groups, scale.reshape(1, d))


def _kv_kernel(h_ref, g_ref, w_ref, cg_ref, cos_ref, sin_ref, k_ref, vt_ref):
    u = _rmsnorm(h_ref[0], g_ref[...]).astype(BF16)
    kv = jnp.dot(u, w_ref[...], preferred_element_type=F32)
    c = _rmsnorm(kv[:, 0:LANES], cg_ref[...])
    rope = kv[:, LANES:2 * LANES] * cos_ref[...] + kv[:, 2 * LANES:] * sin_ref[...]
    k_ref[0, 0, :, 0:LANES] = c.astype(BF16)
    k_ref[0, 0, :, LANES:] = rope.astype(BF16)
    vt_ref[0, 0, 0:KV_LORA_RANK, :] = c.T.astype(BF16)
    extra = (BF16_SUBLANES, c.shape[0])
    ones_row = lax.broadcasted_iota(jnp.int32, extra, 0) == 0
    vt_ref[0, 0, KV_LORA_RANK:, :] = jnp.where(ones_row, 1.0, 0.0).astype(BF16)


def _kv(h, g, w_kv3, ckv_g, cos_k, sin_k, *, tk):
    b, s, d = h.shape
    nkt = s // tk
    return pl.pallas_call(
        _kv_kernel,
        grid=(b, nkt),
        in_specs=[
            pl.BlockSpec((1, tk, d), lambda bi, i: (bi, i, 0)),
            _const_spec((1, d)),
            _const_spec(w_kv3.shape),
            _const_spec((1, KV_LORA_RANK)),
            pl.BlockSpec((tk, LANES), lambda bi, i: (i, 0)),
            pl.BlockSpec((tk, LANES), lambda bi, i: (i, 0)),
        ],
        out_specs=[
            pl.BlockSpec((1, 1, tk, QK_PAD), lambda bi, i: (bi, i, 0, 0)),
            pl.BlockSpec((1, 1, VT_ROWS, tk), lambda bi, i: (bi, i, 0, 0)),
        ],
        out_shape=[
            jax.ShapeDtypeStruct((b, nkt, tk, QK_PAD), BF16),
            jax.ShapeDtypeStruct((b, nkt, VT_ROWS, tk), BF16),
        ],
        compiler_params=_params(("arbitrary", "arbitrary")),
        name="kv",
    )(h, g.reshape(1, d), w_kv3, ckv_g.reshape(1, KV_LORA_RANK), cos_k, sin_k)


def _absorb_q_kernel(uk_ref, uq_ref, out_ref):
    out_ref[0] = lax.dot_general(uk_ref[0], uq_ref[0], (((1,), (1,)), ((), ())),
                                 precision=lax.Precision.HIGHEST,
                                 preferred_element_type=F32)


def _absorb_o_kernel(uv_ref, wo_ref, out_ref):
    out_ref[0] = jnp.dot(uv_ref[0], wo_ref[0], precision=lax.Precision.HIGHEST,
                         preferred_element_type=F32).astype(out_ref.dtype)


def _absorb_q(uk_h, uq_h):
    nh, lat, _ = uk_h.shape
    rank = uq_h.shape[1]
    return pl.pallas_call(
        _absorb_q_kernel,
        grid=(nh,),
        in_specs=[pl.BlockSpec((1,) + uk_h.shape[1:], lambda h: (h, 0, 0)),
                  pl.BlockSpec((1,) + uq_h.shape[1:], lambda h: (h, 0, 0))],
        out_specs=pl.BlockSpec((1, lat, rank), lambda h: (h, 0, 0)),
        out_shape=jax.ShapeDtypeStruct((nh, lat, rank), F32),
        compiler_params=_params(("arbitrary",)),
        name="absorb_q",
    )(uk_h, uq_h)


def _absorb_o(uv_h, wo_h):
    nh, lat, _ = uv_h.shape
    n = wo_h.shape[2]
    return pl.pallas_call(
        _absorb_o_kernel,
        grid=(nh,),
        in_specs=[pl.BlockSpec((1,) + uv_h.shape[1:], lambda h: (h, 0, 0)),
                  pl.BlockSpec((1,) + wo_h.shape[1:], lambda h: (h, 0, 0))],
        out_specs=pl.BlockSpec((1, lat, n), lambda h: (h, 0, 0)),
        out_shape=jax.ShapeDtypeStruct((nh, lat, n), BF16),
        compiler_params=_params(("arbitrary",)),
        name="absorb_o",
    )(uv_h, wo_h)


def _qproj_kernel(h_ref, g_ref, wdq_ref, qg_ref, wq_ref, cos_ref, sin_ref, qt_ref):
    u = _rmsnorm(h_ref[0], g_ref[...]).astype(BF16)
    scale = QK_DIM ** -0.5 * LOG2E
    cq = (_rmsnorm(jnp.dot(u, wdq_ref[...], preferred_element_type=F32),
                   qg_ref[...]) * scale).astype(BF16)
    cos = cos_ref[...]
    sin = sin_ref[...]
    tm = cq.shape[0]
    r0 = KV_LORA_RANK
    for hd in range(wq_ref.shape[0]):
        qt = lax.dot_general(wq_ref[hd], cq, (((1,), (1,)), ((), ())),
                             preferred_element_type=F32)
        x1 = qt[r0:r0 + HALF_ROPE]
        x2 = qt[r0 + HALF_ROPE:r0 + QK_ROPE_DIM]
        qt_ref[0, hd, 0:r0, :] = qt[0:r0].astype(BF16)
        qt_ref[0, hd, r0:r0 + HALF_ROPE, :] = (x1 * cos - x2 * sin).astype(BF16)
        qt_ref[0, hd, r0 + HALF_ROPE:r0 + QK_ROPE_DIM, :] = (x1 * sin + x2 * cos).astype(BF16)
        qt_ref[0, hd, r0 + QK_ROPE_DIM:, :] = jnp.zeros((QK_PAD - r0 - QK_ROPE_DIM, tm), BF16)


def _qproj(h, g, wdq, qg, wq_t, cos_t, sin_t, *, tm=QPROJ_TOKEN_TILE):
    b, s, d = h.shape
    nh = wq_t.shape[0]
    rank = wdq.shape[1]
    tm = min(tm, s)
    return pl.pallas_call(
        _qproj_kernel,
        grid=(b, s // tm),
        in_specs=[
            pl.BlockSpec((1, tm, d), lambda bi, i: (bi, i, 0)),
            _const_spec((1, d)),
            _const_spec(wdq.shape),
            _const_spec((1, rank)),
            _const_spec(wq_t.shape),
            pl.BlockSpec((HALF_ROPE, tm), lambda bi, i: (0, i)),
            pl.BlockSpec((HALF_ROPE, tm), lambda bi, i: (0, i)),
        ],
        out_specs=pl.BlockSpec((1, nh, QK_PAD, tm), lambda bi, i: (bi, 0, 0, i)),
        out_shape=jax.ShapeDtypeStruct((b, nh, QK_PAD, s), BF16),
        compiler_params=_params(("arbitrary", "arbitrary")),
        name="qproj",
    )(h, g.reshape(1, d), wdq, qg.reshape(1, rank), wq_t, cos_t, sin_t)


def _attn_kernel(qt_ref, k_ref, vt_ref, o_ref, acc_ref, m_ref, s_ref, p_ref, cm_ref, *,
                 hb, tq, tk, qsub):
    g = pl.program_id(2)
    kd = (g * qsub * tq) // tk
    units = [(e, j) for e in range(qsub) for j in range(hb)]
    m_ref[...] = jnp.full(m_ref.shape, MASK_VALUE, F32)
    acc_ref[...] = jnp.zeros(acc_ref.shape, F32)

    def qk_unit(k, u, rows=tk):
        e, j = units[u]
        s = jnp.dot(k[0:rows], qt_ref[0, j, :, e * tq:(e + 1) * tq],
                    preferred_element_type=F32)
        s_ref[u, 0:rows] = s.astype(BF16)
        cm_ref[u] = jnp.max(s, axis=0, keepdims=True)

    def softmax_unit(u, keeps, rows=tk):
        s = s_ref[u, 0:rows]
        if keeps is None:
            cm = cm_ref[u]
        else:
            s = jnp.where(keeps[units[u][0]][0:rows], s, MASK_VALUE)
            cm = jnp.max(s, axis=0, keepdims=True).astype(F32)
        m_old = m_ref[u]
        m_new = jnp.maximum(m_old, cm).astype(BF16)
        m_ref[u] = m_new.astype(F32)
        p_ref[u, 0:rows] = jnp.exp2(s - m_new)
        return jnp.exp2(m_old - m_new.astype(F32))

    def pv_unit(vt, u, alpha, rows=tk):
        pv = jnp.dot(vt[:, 0:rows], p_ref[u, 0:rows], preferred_element_type=F32)
        acc_ref[u] = alpha * acc_ref[u] + pv

    def diag_rows(u):
        return (units[u][0] + 1) * tq

    def stage(kt_next, kt_cur, masked=False):
        k = None if kt_next is None else k_ref[0, kt_next]
        vt = vt_ref[0, kt_cur]
        keeps = None
        if masked:
            kpos = kt_cur * tk + lax.broadcasted_iota(jnp.int32, (tk, tq), 0)
            col = lax.broadcasted_iota(jnp.int32, (tk, tq), 1)
            keeps = [kpos <= (g * qsub + e) * tq + col for e in range(qsub)]
        for u in range(len(units)):
            rows = diag_rows(u) if masked else tk
            alpha = softmax_unit(u, keeps, rows)
            if k is not None:
                qk_unit(k, u)
            pv_unit(vt, u, alpha, rows)

    k_diag = k_ref[0, kd]
    for u in range(len(units)):
        qk_unit(k_diag, u, diag_rows(u))
    stage(jnp.maximum(kd - 1, 0), kd, masked=True)

    def body(i, carry):
        idx = 2 * i + 1
        stage(kd - (idx + 1), kd - idx)
        stage(kd - (idx + 2), kd - (idx + 1))
        return carry

    lax.fori_loop(0, jnp.maximum(kd - 1, 0) // 2, body, 0)

    @pl.when(jnp.logical_and(lax.rem(kd, 2) == 0, kd >= 2))
    def _():
        stage(0, 1)

    @pl.when(kd >= 1)
    def _():
        stage(None, 0)

    for u, (e, j) in enumerate(units):
        o = acc_ref[u, 0:V_DIM, :] * (1.0 / acc_ref[u, V_DIM:V_DIM + 1, :])
        o_ref[0, e * tq:(e + 1) * tq, j * V_DIM:(j + 1) * V_DIM] = o.T.astype(o_ref.dtype)


def _attn(qt, k, vt, *, hb=ATTN_HEADS_PER_STEP, tq=ATTN_QUERY_TILE):
    b, nh, _, s = qt.shape
    nkt, tk = k.shape[1], k.shape[2]
    qsub = tk // tq
    assert tk == qsub * tq and vt.shape[2] == VT_ROWS
    n_units = hb * qsub
    return pl.pallas_call(
        functools.partial(_attn_kernel, hb=hb, tq=tq, tk=tk, qsub=qsub),
        grid=(b, nh // hb, s // tk),
        in_specs=[
            pl.BlockSpec((1, hb, QK_PAD, tk), lambda bi, hg, gi: (bi, hg, 0, gi)),
            pl.BlockSpec((1, nkt, tk, QK_PAD), lambda bi, hg, gi: (bi, 0, 0, 0),
                         pipeline_mode=pl.Buffered(1)),
            pl.BlockSpec((1, nkt, VT_ROWS, tk), lambda bi, hg, gi: (bi, 0, 0, 0),
                         pipeline_mode=pl.Buffered(1)),
        ],
        out_specs=pl.BlockSpec((1, tk, hb * V_DIM), lambda bi, hg, gi: (bi, gi, hg)),
        out_shape=jax.ShapeDtypeStruct((b, s, nh * V_DIM), BF16),
        scratch_shapes=[
            pltpu.VMEM((n_units, VT_ROWS, tq), F32),
            pltpu.VMEM((n_units, 1, tq), F32),
            pltpu.VMEM((n_units, tk, tq), BF16),
            pltpu.VMEM((n_units, tk, tq), BF16),
            pltpu.VMEM((n_units, 1, tq), F32),
        ],
        compiler_params=_params(("arbitrary", "arbitrary", "arbitrary")),
        name="attn",
    )(qt, k, vt)


def _rope_tables(seq):
    pos = jnp.arange(seq, dtype=F32)
    inv_freq = ROPE_THETA ** (-jnp.arange(0, QK_ROPE_DIM, 2, dtype=F32) / QK_ROPE_DIM)
    ang = pos[:, None] * inv_freq[None, :]
    return jnp.cos(ang), jnp.sin(ang)


def kernel(x, ffn_pre_norm, ffn_pre_wg, ffn_pre_wu, ffn_pre_wd, mix_norm, ffn_post_norm, ffn_post_wg, ffn_post_wu, ffn_post_wd, pool_w, pool_scale, kv_in_norm, w_dkv, ckv_norm, w_uk, w_uv, q_lora_norm, w_dq, w_uq, w_o, final_norm):
    b, s, d = x.shape
    depth = ffn_pre_norm.shape[0]
    n_a = pool_w.shape[0]
    t = b * s
    tq = min(ATTN_QUERY_TILE, s)
    tk = min(ATTN_KEY_TILE, s)

    pre = [w.astype(BF16) for w in (ffn_pre_wg, ffn_pre_wu, ffn_pre_wd)]
    post = [w.astype(BF16) for w in (ffn_post_wg, ffn_post_wu, ffn_post_wd)]
    pre_g = ffn_pre_norm.reshape(depth, 1, d)
    post_g = ffn_post_norm.reshape(depth, 1, d)
    pool_w_bf = pool_w.astype(BF16)

    cos, sin = _rope_tables(s)
    pad = jnp.zeros((s, LANES - QK_ROPE_DIM), F32)
    cos_k = jnp.concatenate([cos, cos, pad], axis=1)
    sin_k = jnp.concatenate([sin, sin, pad], axis=1)
    cos_t, sin_t = cos.T, sin.T

    wr = w_dkv[:, KV_LORA_RANK:]
    wpad = jnp.zeros((d, LANES - QK_ROPE_DIM), F32)
    w_kv3 = jnp.concatenate(
        [w_dkv[:, :KV_LORA_RANK], wr, wpad,
         -wr[:, HALF_ROPE:], wr[:, :HALF_ROPE], wpad], axis=1).astype(BF16)

    uk_h = jnp.transpose(w_uk, (1, 0, 2))
    uv_h = jnp.transpose(w_uv, (1, 0, 2))
    n_b = w_dq.shape[0]
    wq_ts, wo_abs = [], []
    for j in range(n_b):
        uq_h = jnp.transpose(w_uq[j], (1, 0, 2))
        a_nope = _absorb_q(uk_h, uq_h[:, :, :QK_NOPE_DIM])
        a_rope = jnp.transpose(uq_h[:, :, QK_NOPE_DIM:], (0, 2, 1))
        wq_ts.append(jnp.concatenate([a_nope, a_rope], axis=1).astype(BF16))
        wo_h = w_o[j].reshape(N_HEADS, V_DIM, d)
        wo_abs.append(_absorb_o(uv_h, wo_h).reshape(N_HEADS * KV_LORA_RANK, d))

    h = x.reshape(t, d)
    k_lat = vt_lat = None
    o_prev = wo_prev = None
    for l in range(depth):
        h = _ffn(h, l, pre_g, *pre)
        if l < n_a:
            h = _pool(h.reshape(b, s, d), mix_norm[l], pool_w_bf[l], pool_scale[l]).reshape(t, d)
            o_prev = wo_prev = None
        else:
            j = l - n_a
            qt = _qproj(h.reshape(b, s, d), mix_norm[l], w_dq[j].astype(BF16),
                        q_lora_norm[j], wq_ts[j], cos_t, sin_t)
            o_prev = _attn(qt, k_lat, vt_lat, tq=tq).reshape(t, N_HEADS * V_DIM)
            wo_prev = wo_abs[j]
        h = _ffn(h, l, post_g, *post, o2d=o_prev, wo=wo_prev,
                 final_g=final_norm if l == depth - 1 else None)
        if l == n_a - 1:
            k_lat, vt_lat = _kv(h.reshape(b, s, d), kv_in_norm, w_kv3, ckv_norm,
                                cos_k, sin_k, tk=tk)
    return h.reshape(b, s, d)
```

```python
import functools

import jax
import jax.numpy as jnp
from jax import lax
from jax.experimental import pallas as pl
from jax.experimental.pallas import tpu as pltpu

RMS_EPS = 1e-6
POOL_WINDOWS = (2, 4, 8, 16)
MAX_WINDOW = max(POOL_WINDOWS)
N_HEADS = 16
QK_NOPE_DIM = 128
QK_ROPE_DIM = 64
QK_DIM = QK_NOPE_DIM + QK_ROPE_DIM
V_DIM = 128
KV_LORA_RANK = 128
ROPE_THETA = 10000.0
HALF_ROPE = QK_ROPE_DIM // 2

VMEM_LIMIT_BYTES = 56 * 1024 * 1024
LANES = 128
QK_PAD = 256
BF16_SUBLANES = 16
VT_ROWS = KV_LORA_RANK + BF16_SUBLANES
LOG2E = 1.4426950408889634
MASK_VALUE = -1e30

FFN_TOKEN_TILE = 1024
FFN_HIDDEN_CHUNK = 256
POOL_TOKEN_TILE = 512
QPROJ_TOKEN_TILE = 1024
ATTN_QUERY_TILE = 256
ATTN_KEY_TILE = 512
ATTN_HEADS_PER_STEP = 16

BF16 = jnp.bfloat16
F32 = jnp.float32


def _rmsnorm(x, g):
    ms = jnp.mean(x * x, axis=-1, keepdims=True)
    return x * lax.rsqrt(ms + RMS_EPS) * g


def _const_spec(shape):
    nd = len(shape)
    return pl.BlockSpec(shape, lambda *_: (0,) * nd, pipeline_mode=pl.Buffered(1))


def _layer_spec(stacked_shape, layer):
    rest = tuple(stacked_shape[1:])
    zeros = (0,) * len(rest)
    return pl.BlockSpec((None,) + rest, lambda *_: (layer,) + zeros,
                        pipeline_mode=pl.Buffered(1))


def _params(sem):
    return pltpu.CompilerParams(dimension_semantics=sem,
                                vmem_limit_bytes=VMEM_LIMIT_BYTES)


def _ffn_kernel(*refs, tf, with_oproj, with_final):
    it = iter(refs)
    h_ref = next(it)
    if with_oproj:
        o_ref = next(it)
        wo_ref = next(it)
    g_ref, wg_ref, wu_ref, wd_ref = next(it), next(it), next(it), next(it)
    if with_final:
        fg_ref = next(it)
    out_ref = next(it)
    a_ref = next(it)

    x = h_ref[...]
    if with_oproj:
        x = x + jnp.dot(o_ref[...], wo_ref[...], preferred_element_type=F32)
    u = _rmsnorm(x, g_ref[...]).astype(BF16)
    d_ff = wg_ref.shape[1]
    for c in range(d_ff // tf):
        sl = slice(c * tf, (c + 1) * tf)
        gate = jnp.dot(u, wg_ref[:, sl], preferred_element_type=F32)
        up = jnp.dot(u, wu_ref[:, sl], preferred_element_type=F32)
        a_ref[:, sl] = (gate * jax.nn.sigmoid(gate) * up).astype(BF16)
    y = jnp.dot(a_ref[...], wd_ref[...], preferred_element_type=F32)
    res = x + 0.5 * y
    if with_final:
        res = _rmsnorm(res, fg_ref[...])
    out_ref[...] = res


def _ffn(h2d, layer, g, wg, wu, wd, *, o2d=None, wo=None, final_g=None, tm=FFN_TOKEN_TILE, tf=FFN_HIDDEN_CHUNK):
    t, d = h2d.shape
    d_ff = wg.shape[2]
    tm = min(tm, t)
    row = lambda i: (i, 0)
    args = [h2d]
    specs = [pl.BlockSpec((tm, d), row)]
    if o2d is not None:
        args += [o2d, wo]
        specs += [pl.BlockSpec((tm, o2d.shape[1]), row), _const_spec(wo.shape)]
    args += [g, wg, wu, wd]
    specs += [_layer_spec(a.shape, layer) for a in (g, wg, wu, wd)]
    if final_g is not None:
        args.append(final_g.reshape(1, d))
        specs.append(_const_spec((1, d)))
    kern = functools.partial(_ffn_kernel, tf=tf, with_oproj=o2d is not None,
                             with_final=final_g is not None)
    return pl.pallas_call(
        kern,
        grid=(t // tm,),
        in_specs=specs,
        out_specs=pl.BlockSpec((tm, d), row),
        out_shape=jax.ShapeDtypeStruct((t, d), F32),
        scratch_shapes=[pltpu.VMEM((tm, d_ff), BF16)],
        compiler_params=_params(("arbitrary",)),
        name="ffn",
    )(*args)


def _pool_kernel(h_ref, halo_ref, g_ref, w_ref, sc_ref, out_ref, ext_ref, *, ts):
    i = pl.program_id(1)
    g = g_ref[...]
    x = h_ref[0]
    u = _rmsnorm(x, g)
    halo = _rmsnorm(halo_ref[0], g)
    ext_ref[0:MAX_WINDOW, :] = jnp.where(i > 0, halo, 0.0)
    ext_ref[MAX_WINDOW:, :] = u
    pos = i * ts + lax.broadcasted_iota(jnp.int32, (ts, 1), 0)
    dg = w_ref.shape[1]
    for gi, w in enumerate(POOL_WINDOWS):
        cs = slice(gi * dg, (gi + 1) * dg)
        win = ext_ref[:, cs]
        shift = 1
        while shift < w:
            win = win + pltpu.roll(win, shift, 0)
            shift *= 2
        win = win[MAX_WINDOW:]
        ug = u[:, cs]
        inv_count = 1.0 / jnp.minimum(pos + 1, w).astype(F32)
        y = (win * inv_count - ug).astype(BF16)
        z = jnp.dot(y, w_ref[gi], preferred_element_type=F32)
        out_ref[0, :, cs] = x[:, cs] + z * sc_ref[:, cs]


def _pool(h, g, w_groups, scale, *, ts=POOL_TOKEN_TILE):
    b, s, d = h.shape
    ts = min(ts, s)
    halo_blocks = ts // MAX_WINDOW
    return pl.pallas_call(
        functools.partial(_pool_kernel, ts=ts),
        grid=(b, s // ts),
        in_specs=[
            pl.BlockSpec((1, ts, d), lambda bi, i: (bi, i, 0)),
            pl.BlockSpec((1, MAX_WINDOW, d),
                         lambda bi, i: (bi, jnp.maximum(i * halo_blocks - 1, 0), 0)),
            _const_spec((1, d)),
            _const_spec(w_groups.shape),
            _const_spec((1, d)),
        ],
        out_specs=pl.BlockSpec((1, ts, d), lambda bi, i: (bi, i, 0)),
        out_shape=jax.ShapeDtypeStruct((b, s, d), F32),
        scratch_shapes=[pltpu.VMEM((ts + MAX_WINDOW, d), F32)],
        compiler_params=_params(("arbitrary", "arbitrary")),
        name="pool",
    )(h, h, g.reshape(1, d), w_groups, scale.reshape(1, d))


def _kv_kernel(h_ref, g_ref, w_ref, cg_ref, cos_ref, sin_ref, k_ref, vt_ref):
    u = _rmsnorm(h_ref[0], g_ref[...]).astype(BF16)
    kv = jnp.dot(u, w_ref[...], preferred_element_type=F32)
    c = _rmsnorm(kv[:, 0:LANES], cg_ref[...])
    rope = kv[:, LANES:2 * LANES] * cos_ref[...] + kv[:, 2 * LANES:] * sin_ref[...]
    k_ref[0, 0, :, 0:LANES] = c.astype(BF16)
    k_ref[0, 0, :, LANES:] = rope.astype(BF16)
    vt_ref[0, 0, 0:KV_LORA_RANK, :] = c.T.astype(BF16)
    extra = (BF16_SUBLANES, c.shape[0])
    ones_row = lax.broadcasted_iota(jnp.int32, extra, 0) == 0
    vt_ref[0, 0, KV_LORA_RANK:, :] = jnp.where(ones_row, 1.0, 0.0).astype(BF16)


def _kv(h, g, w_kv3, ckv_g, cos_k, sin_k, *, tk):
    b, s, d = h.shape
    nkt = s // tk
    return pl.pallas_call(
        _kv_kernel,
        grid=(b, nkt),
        in_specs=[
            pl.BlockSpec((1, tk, d), lambda bi, i: (bi, i, 0)),
            _const_spec((1, d)),
            _const_spec(w_kv3.shape),
            _const_spec((1, KV_LORA_RANK)),
            pl.BlockSpec((tk, LANES), lambda bi, i: (i, 0)),
            pl.BlockSpec((tk, LANES), lambda bi, i: (i, 0)),
        ],
        out_specs=[
            pl.BlockSpec((1, 1, tk, QK_PAD), lambda bi, i: (bi, i, 0, 0)),
            pl.BlockSpec((1, 1, VT_ROWS, tk), lambda bi, i: (bi, i, 0, 0)),
        ],
        out_shape=[
            jax.ShapeDtypeStruct((b, nkt, tk, QK_PAD), BF16),
            jax.ShapeDtypeStruct((b, nkt, VT_ROWS, tk), BF16),
        ],
        compiler_params=_params(("arbitrary", "arbitrary")),
        name="kv",
    )(h, g.reshape(1, d), w_kv3, ckv_g.reshape(1, KV_LORA_RANK), cos_k, sin_k)


def _absorb_q_kernel(uk_ref, uq_ref, out_ref):
    out_ref[0] = lax.dot_general(uk_ref[0], uq_ref[0], (((1,), (1,)), ((), ())),
                                 precision=lax.Precision.HIGHEST,
                                 preferred_element_type=F32)


def _absorb_o_kernel(uv_ref, wo_ref, out_ref):
    out_ref[0] = jnp.dot(uv_ref[0], wo_ref[0], precision=lax.Precision.HIGHEST,
                         preferred_element_type=F32).astype(out_ref.dtype)


def _absorb_q(uk_h, uq_h):
    nh, lat, _ = uk_h.shape
    rank = uq_h.shape[1]
    return pl.pallas_call(
        _absorb_q_kernel,
        grid=(nh,),
        in_specs=[pl.BlockSpec((1,) + uk_h.shape[1:], lambda h: (h, 0, 0)),
                  pl.BlockSpec((1,) + uq_h.shape[1:], lambda h: (h, 0, 0))],
        out_specs=pl.BlockSpec((1, lat, rank), lambda h: (h, 0, 0)),
        out_shape=jax.ShapeDtypeStruct((nh, lat, rank), F32),
        compiler_params=_params(("arbitrary",)),
        name="absorb_q",
    )(uk_h, uq_h)


def _absorb_o(uv_h, wo_h):
    nh, lat, _ = uv_h.shape
    n = wo_h.shape[2]
    return pl.pallas_call(
        _absorb_o_kernel,
        grid=(nh,),
        in_specs=[pl.BlockSpec((1,) + uv_h.shape[1:], lambda h: (h, 0, 0)),
                  pl.BlockSpec((1,) + wo_h.shape[1:], lambda h: (h, 0, 0))],
        out_specs=pl.BlockSpec((1, lat, n), lambda h: (h, 0, 0)),
        out_shape=jax.ShapeDtypeStruct((nh, lat, n), BF16),
        compiler_params=_params(("arbitrary",)),
        name="absorb_o",
    )(uv_h, wo_h)


def _qproj_kernel(h_ref, g_ref, wdq_ref, qg_ref, wq_ref, cos_ref, sin_ref, qt_ref):
    u = _rmsnorm(h_ref[0], g_ref[...]).astype(BF16)
    scale = QK_DIM ** -0.5 * LOG2E
    cq = (_rmsnorm(jnp.dot(u, wdq_ref[...], preferred_element_type=F32),
                   qg_ref[...]) * scale).astype(BF16)
    cos = cos_ref[...]
    sin = sin_ref[...]
    tm = cq.shape[0]
    r0 = KV_LORA_RANK
    for hd in range(wq_ref.shape[0]):
        qt = lax.dot_general(wq_ref[hd], cq, (((1,), (1,)), ((), ())),
                             preferred_element_type=F32)
        x1 = qt[r0:r0 + HALF_ROPE]
        x2 = qt[r0 + HALF_ROPE:r0 + QK_ROPE_DIM]
        qt_ref[0, hd, 0:r0, :] = qt[0:r0].astype(BF16)
        qt_ref[0, hd, r0:r0 + HALF_ROPE, :] = (x1 * cos - x2 * sin).astype(BF16)
        qt_ref[0, hd, r0 + HALF_ROPE:r0 + QK_ROPE_DIM, :] = (x1 * sin + x2 * cos).astype(BF16)
        qt_ref[0, hd, r0 + QK_ROPE_DIM:, :] = jnp.zeros((QK_PAD - r0 - QK_ROPE_DIM, tm), BF16)


def _qproj(h, g, wdq, qg, wq_t, cos_t, sin_t, *, tm=QPROJ_TOKEN_TILE):
    b, s, d = h.shape
    nh = wq_t.shape[0]
    rank = wdq.shape[1]
    tm = min(tm, s)
    return pl.pallas_call(
        _qproj_kernel,
        grid=(b, s // tm),
        in_specs=[
            pl.BlockSpec((1, tm, d), lambda bi, i: (bi, i, 0)),
            _const_spec((1, d)),
            _const_spec(wdq.shape),
            _const_spec((1, rank)),
            _const_spec(wq_t.shape),
            pl.BlockSpec((HALF_ROPE, tm), lambda bi, i: (0, i)),
            pl.BlockSpec((HALF_ROPE, tm), lambda bi, i: (0, i)),
        ],
        out_specs=pl.BlockSpec((1, nh, QK_PAD, tm), lambda bi, i: (bi, 0, 0, i)),
        out_shape=jax.ShapeDtypeStruct((b, nh, QK_PAD, s), BF16),
        compiler_params=_params(("arbitrary", "arbitrary")),
        name="qproj",
    )(h, g.reshape(1, d), wdq, qg.reshape(1, rank), wq_t, cos_t, sin_t)


def _attn_kernel(qt_ref, k_ref, vt_ref, o_ref, acc_ref, m_ref, s_ref, cm_ref, *,
                 hb, tq, tk, qsub):
    g = pl.program_id(2)
    kd = (g * qsub * tq) // tk
    units = [(e, j) for e in range(qsub) for j in range(hb)]
    m_ref[...] = jnp.full(m_ref.shape, MASK_VALUE, F32)
    acc_ref[...] = jnp.zeros(acc_ref.shape, F32)

    def diag_rows(u):
        return (units[u][0] + 1) * tq

    def qk_unit(k, slot, u, rows=tk):
        e, j = units[u]
        s = jnp.dot(k[0:rows], qt_ref[0, j, :, e * tq:(e + 1) * tq],
                    preferred_element_type=F32)
        s_ref[slot, u, 0:rows] = s.astype(BF16)
        cm_ref[slot, u] = jnp.max(s, axis=0, keepdims=True)

    def softmax_pv_unit(vt, slot, u, keeps, rows=tk):
        s = s_ref[slot, u, 0:rows]
        if keeps is None:
            cm = cm_ref[slot, u]
        else:
            s = jnp.where(keeps[units[u][0]][0:rows], s, MASK_VALUE)
            cm = jnp.max(s, axis=0, keepdims=True).astype(F32)
        m_old = m_ref[u]
        m_new = jnp.maximum(m_old, cm).astype(BF16)
        m_ref[u] = m_new.astype(F32)
        alpha = jnp.exp2(m_old - m_new.astype(F32))
        p = jnp.exp2(s - m_new)
        pv = jnp.dot(vt[:, 0:rows], p, preferred_element_type=F32)
        acc_ref[u] = alpha * acc_ref[u] + pv

    def stage(kt_next, kt_cur, cur_slot, masked=False):
        k = None if kt_next is None else k_ref[0, kt_next]
        vt = vt_ref[0, kt_cur]
        keeps = None
        if masked:
            kpos = kt_cur * tk + lax.broadcasted_iota(jnp.int32, (tk, tq), 0)
            col = lax.broadcasted_iota(jnp.int32, (tk, tq), 1)
            keeps = [kpos <= (g * qsub + e) * tq + col for e in range(qsub)]
        for u in range(len(units)):
            if k is not None:
                qk_unit(k, 1 - cur_slot, u)
            softmax_pv_unit(vt, cur_slot, u, keeps, diag_rows(u) if masked else tk)

    k_diag = k_ref[0, kd]
    for u in range(len(units)):
        qk_unit(k_diag, 0, u, diag_rows(u))
    stage(jnp.maximum(kd - 1, 0), kd, 0, masked=True)

    def body(i, carry):
        idx = 2 * i + 1
        stage(kd - (idx + 1), kd - idx, 1)
        stage(kd - (idx + 2), kd - (idx + 1), 0)
        return carry

    lax.fori_loop(0, jnp.maximum(kd - 1, 0) // 2, body, 0)

    @pl.when(jnp.logical_and(lax.rem(kd, 2) == 0, kd >= 2))
    def _():
        stage(0, 1, 1)
        stage(None, 0, 0)

    @pl.when(lax.rem(kd, 2) == 1)
    def _():
        stage(None, 0, 1)

    for u, (e, j) in enumerate(units):
        o = acc_ref[u, 0:V_DIM, :] * (1.0 / acc_ref[u, V_DIM:V_DIM + 1, :])
        o_ref[0, e * tq:(e + 1) * tq, j * V_DIM:(j + 1) * V_DIM] = o.T.astype(o_ref.dtype)


def _attn(qt, k, vt, *, hb=ATTN_HEADS_PER_STEP, tq=ATTN_QUERY_TILE):
    b, nh, _, s = qt.shape
    nkt, tk = k.shape[1], k.shape[2]
    qsub = tk // tq
    assert tk == qsub * tq and vt.shape[2] == VT_ROWS
    n_units = hb * qsub
    return pl.pallas_call(
        functools.partial(_attn_kernel, hb=hb, tq=tq, tk=tk, qsub=qsub),
        grid=(b, nh // hb, s // tk),
        in_specs=[
            pl.BlockSpec((1, hb, QK_PAD, tk), lambda bi, hg, gi: (bi, hg, 0, gi)),
            pl.BlockSpec((1, nkt, tk, QK_PAD), lambda bi, hg, gi: (bi, 0, 0, 0),
                         pipeline_mode=pl.Buffered(1)),
            pl.BlockSpec((1, nkt, VT_ROWS, tk), lambda bi, hg, gi: (bi, 0, 0, 0),
                         pipeline_mode=pl.Buffered(1)),
        ],
        out_specs=pl.BlockSpec((1, tk, hb * V_DIM), lambda bi, hg, gi: (bi, gi, hg)),
        out_shape=jax.ShapeDtypeStruct((b, s, nh * V_DIM), BF16),
        scratch_shapes=[
            pltpu.VMEM((n_units, VT_ROWS, tq), F32),
            pltpu.VMEM((n_units, 1, tq), F32),
            pltpu.VMEM((2, n_units, tk, tq), BF16),
            pltpu.VMEM((2, n_units, 1, tq), F32),
        ],
        compiler_params=_params(("arbitrary", "arbitrary", "arbitrary")),
        name="attn",
    )(qt, k, vt)


def _rope_tables(seq):
    pos = jnp.arange(seq, dtype=F32)
    inv_freq = ROPE_THETA ** (-jnp.arange(0, QK_ROPE_DIM, 2, dtype=F32) / QK_ROPE_DIM)
    ang = pos[:, None] * inv_freq[None, :]
    return jnp.cos(ang), jnp.sin(ang)


def kernel(x, ffn_pre_norm, ffn_pre_wg, ffn_pre_wu, ffn_pre_wd, mix_norm, ffn_post_norm, ffn_post_wg, ffn_post_wu, ffn_post_wd, pool_w, pool_scale, kv_in_norm, w_dkv, ckv_norm, w_uk, w_uv, q_lora_norm, w_dq, w_uq, w_o, final_norm):
    b, s, d = x.shape
    depth = ffn_pre_norm.shape[0]
    n_a = pool_w.shape[0]
    t = b * s
    tq = min(ATTN_QUERY_TILE, s)
    tk = min(ATTN_KEY_TILE, s)

    pre = [w.astype(BF16) for w in (ffn_pre_wg, ffn_pre_wu, ffn_pre_wd)]
    post = [w.astype(BF16) for w in (ffn_post_wg, ffn_post_wu, ffn_post_wd)]
    pre_g = ffn_pre_norm.reshape(depth, 1, d)
    post_g = ffn_post_norm.reshape(depth, 1, d)
    pool_w_bf = pool_w.astype(BF16)

    cos, sin = _rope_tables(s)
    pad = jnp.zeros((s, LANES - QK_ROPE_DIM), F32)
    cos_k = jnp.concatenate([cos, cos, pad], axis=1)
    sin_k = jnp.concatenate([sin, sin, pad], axis=1)
    cos_t, sin_t = cos.T, sin.T

    wr = w_dkv[:, KV_LORA_RANK:]
    wpad = jnp.zeros((d, LANES - QK_ROPE_DIM), F32)
    w_kv3 = jnp.concatenate(
        [w_dkv[:, :KV_LORA_RANK], wr, wpad,
         -wr[:, HALF_ROPE:], wr[:, :HALF_ROPE], wpad], axis=1).astype(BF16)

    uk_h = jnp.transpose(w_uk, (1, 0, 2))
    uv_h = jnp.transpose(w_uv, (1, 0, 2))
    n_b = w_dq.shape[0]
    wq_ts, wo_abs = [], []
    for j in range(n_b):
        uq_h = jnp.transpose(w_uq[j], (1, 0, 2))
        a_nope = _absorb_q(uk_h, uq_h[:, :, :QK_NOPE_DIM])
        a_rope = jnp.transpose(uq_h[:, :, QK_NOPE_DIM:], (0, 2, 1))
        wq_ts.append(jnp.concatenate([a_nope, a_rope], axis=1).astype(BF16))
        wo_h = w_o[j].reshape(N_HEADS, V_DIM, d)
        wo_abs.append(_absorb_o(uv_h, wo_h).reshape(N_HEADS * KV_LORA_RANK, d))

    h = x.reshape(t, d)
    k_lat = vt_lat = None
    o_prev = wo_prev = None
    for l in range(depth):
        h = _ffn(h, l, pre_g, *pre)
        if l < n_a:
            h = _pool(h.reshape(b, s, d), mix_norm[l], pool_w_bf[l], pool_scale[l]).reshape(t, d)
            o_prev = wo_prev = None
        else:
            j = l - n_a
            qt = _qproj(h.reshape(b, s, d), mix_norm[l], w_dq[j].astype(BF16),
                        q_lora_norm[j], wq_ts[j], cos_t, sin_t)
            o_prev = _attn(qt, k_lat, vt_lat, tq=tq).reshape(t, N_HEADS * V_DIM)
            wo_prev = wo_abs[j]
        h = _ffn(h, l, post_g, *post, o2d=o_prev, wo=wo_prev,
                 final_g=final_norm if l == depth - 1 else None)
        if l == n_a - 1:
            k_lat, vt_lat = _kv(h.reshape(b, s, d), kv_in_norm, w_kv3, ckv_norm,
                                cos_k, sin_k, tk=tk)
    return h.reshape(b, s, d)
```

```python
import functools

import jax
import jax.numpy as jnp
from jax import lax
from jax.experimental import pallas as pl
from jax.experimental.pallas import tpu as pltpu

RMS_EPS = 1e-6
POOL_WINDOWS = (2, 4, 8, 16)
MAX_WINDOW = max(POOL_WINDOWS)
N_HEADS = 16
QK_NOPE_DIM = 128
QK_ROPE_DIM = 64
QK_DIM = QK_NOPE_DIM + QK_ROPE_DIM
V_DIM = 128
KV_LORA_RANK = 128
ROPE_THETA = 10000.0
HALF_ROPE = QK_ROPE_DIM // 2

VMEM_LIMIT_BYTES = 56 * 1024 * 1024
LANES = 128
QK_PAD = 256
BF16_SUBLANES = 16
VT_ROWS = KV_LORA_RANK + BF16_SUBLANES
LOG2E = 1.4426950408889634
MASK_VALUE = -1e30

FFN_TOKEN_TILE = 1024
FFN_HIDDEN_CHUNK = 256
POOL_TOKEN_TILE = 512
QPROJ_TOKEN_TILE = 1024
ATTN_QUERY_TILE = 256
ATTN_KEY_TILE = 512
ATTN_HEADS_PER_STEP = 16

BF16 = jnp.bfloat16
F32 = jnp.float32


def _rmsnorm(x, g):
    ms = jnp.mean(x * x, axis=-1, keepdims=True)
    return x * lax.rsqrt(ms + RMS_EPS) * g


def _const_spec(shape):
    nd = len(shape)
    return pl.BlockSpec(shape, lambda *_: (0,) * nd, pipeline_mode=pl.Buffered(1))


def _layer_spec(stacked_shape, layer):
    rest = tuple(stacked_shape[1:])
    zeros = (0,) * len(rest)
    return pl.BlockSpec((None,) + rest, lambda *_: (layer,) + zeros,
                        pipeline_mode=pl.Buffered(1))


def _params(sem):
    return pltpu.CompilerParams(dimension_semantics=sem,
                                vmem_limit_bytes=VMEM_LIMIT_BYTES)


def _ffn_kernel(*refs, tf, with_oproj, with_final):
    it = iter(refs)
    h_ref = next(it)
    if with_oproj:
        o_ref = next(it)
        wo_ref = next(it)
    g_ref, wg_ref, wu_ref, wd_ref = next(it), next(it), next(it), next(it)
    if with_final:
        fg_ref = next(it)
    out_ref = next(it)
    a_ref = next(it)

    x = h_ref[...]
    if with_oproj:
        x = x + jnp.dot(o_ref[...], wo_ref[...], preferred_element_type=F32)
    u = _rmsnorm(x, g_ref[...]).astype(BF16)
    d_ff = wg_ref.shape[1]
    for c in range(d_ff // tf):
        sl = slice(c * tf, (c + 1) * tf)
        gate = jnp.dot(u, wg_ref[:, sl], preferred_element_type=F32)
        up = jnp.dot(u, wu_ref[:, sl], preferred_element_type=F32)
        a_ref[:, sl] = (gate * jax.nn.sigmoid(gate) * up).astype(BF16)
    y = jnp.dot(a_ref[...], wd_ref[...], preferred_element_type=F32)
    res = x + 0.5 * y
    if with_final:
        res = _rmsnorm(res, fg_ref[...])
    out_ref[...] = res


def _ffn(h2d, layer, g, wg, wu, wd, *, o2d=None, wo=None, final_g=None, tm=FFN_TOKEN_TILE, tf=FFN_HIDDEN_CHUNK):
    t, d = h2d.shape
    d_ff = wg.shape[2]
    tm = min(tm, t)
    assert t % tm == 0 and d_ff % tf == 0, (t, tm, d_ff, tf)
    row = lambda i: (i, 0)
    args = [h2d]
    specs = [pl.BlockSpec((tm, d), row)]
    if o2d is not None:
        args += [o2d, wo]
        specs += [pl.BlockSpec((tm, o2d.shape[1]), row), _const_spec(wo.shape)]
    args += [g, wg, wu, wd]
    specs += [_layer_spec(a.shape, layer) for a in (g, wg, wu, wd)]
    if final_g is not None:
        args.append(final_g.reshape(1, d))
        specs.append(_const_spec((1, d)))
    kern = functools.partial(_ffn_kernel, tf=tf, with_oproj=o2d is not None,
                             with_final=final_g is not None)
    return pl.pallas_call(
        kern,
        grid=(t // tm,),
        in_specs=specs,
        out_specs=pl.BlockSpec((tm, d), row),
        out_shape=jax.ShapeDtypeStruct((t, d), F32),
        scratch_shapes=[pltpu.VMEM((tm, d_ff), BF16)],
        compiler_params=_params(("arbitrary",)),
        name="ffn",
    )(*args)


def _pool_kernel(h_ref, halo_ref, g_ref, w_ref, sc_ref, out_ref, ext_ref, *, ts):
    i = pl.program_id(1)
    g = g_ref[...]
    x = h_ref[0]
    u = _rmsnorm(x, g)
    halo = _rmsnorm(halo_ref[0], g)
    ext_ref[0:MAX_WINDOW, :] = jnp.where(i > 0, halo, 0.0)
    ext_ref[MAX_WINDOW:, :] = u
    pos = i * ts + lax.broadcasted_iota(jnp.int32, (ts, 1), 0)
    dg = w_ref.shape[1]
    for gi, w in enumerate(POOL_WINDOWS):
        cs = slice(gi * dg, (gi + 1) * dg)
        win = ext_ref[:, cs]
        shift = 1
        while shift < w:
            win = win + pltpu.roll(win, shift, 0)
            shift *= 2
        win = win[MAX_WINDOW:]
        ug = u[:, cs]
        inv_count = 1.0 / jnp.minimum(pos + 1, w).astype(F32)
        y = (win * inv_count - ug).astype(BF16)
        z = jnp.dot(y, w_ref[gi], preferred_element_type=F32)
        out_ref[0, :, cs] = x[:, cs] + z * sc_ref[:, cs]


def _pool(h, g, w_groups, scale, *, ts=POOL_TOKEN_TILE):
    b, s, d = h.shape
    ts = min(ts, s)
    assert all(w & (w - 1) == 0 for w in POOL_WINDOWS)
    assert s % ts == 0 and ts % MAX_WINDOW == 0 and len(POOL_WINDOWS) == w_groups.shape[0]
    halo_blocks = ts // MAX_WINDOW
    return pl.pallas_call(
        functools.partial(_pool_kernel, ts=ts),
        grid=(b, s // ts),
        in_specs=[
            pl.BlockSpec((1, ts, d), lambda bi, i: (bi, i, 0)),
            pl.BlockSpec((1, MAX_WINDOW, d),
                         lambda bi, i: (bi, jnp.maximum(i * halo_blocks - 1, 0), 0)),
            _const_spec((1, d)),
            _const_spec(w_groups.shape),
            _const_spec((1, d)),
        ],
        out_specs=pl.BlockSpec((1, ts, d), lambda bi, i: (bi, i, 0)),
        out_shape=jax.ShapeDtypeStruct((b, s, d), F32),
        scratch_shapes=[pltpu.VMEM((ts + MAX_WINDOW, d), F32)],
        compiler_params=_params(("arbitrary", "arbitrary")),
        name="pool",
    )(h, h, g.reshape(1, d), w_groups, scale.reshape(1, d))


def _kv_kernel(h_ref, g_ref, w_ref, cg_ref, cos_ref, sin_ref, k_ref, vt_ref):
    u = _rmsnorm(h_ref[0], g_ref[...]).astype(BF16)
    kv = jnp.dot(u, w_ref[...], preferred_element_type=F32)
    c = _rmsnorm(kv[:, 0:LANES], cg_ref[...])
    rope = kv[:, LANES:2 * LANES] * cos_ref[...] + kv[:, 2 * LANES:] * sin_ref[...]
    k_ref[0, 0, :, 0:LANES] = c.astype(BF16)
    k_ref[0, 0, :, LANES:] = rope.astype(BF16)
    vt_ref[0, 0, 0:KV_LORA_RANK, :] = c.T.astype(BF16)
    extra = (BF16_SUBLANES, c.shape[0])
    ones_row = lax.broadcasted_iota(jnp.int32, extra, 0) == 0
    vt_ref[0, 0, KV_LORA_RANK:, :] = jnp.where(ones_row, 1.0, 0.0).astype(BF16)


def _kv(h, g, w_kv3, ckv_g, cos_k, sin_k, *, tk):
    b, s, d = h.shape
    assert s % tk == 0, (s, tk)
    nkt = s // tk
    return pl.pallas_call(
        _kv_kernel,
        grid=(b, nkt),
        in_specs=[
            pl.BlockSpec((1, tk, d), lambda bi, i: (bi, i, 0)),
            _const_spec((1, d)),
            _const_spec(w_kv3.shape),
            _const_spec((1, KV_LORA_RANK)),
            pl.BlockSpec((tk, LANES), lambda bi, i: (i, 0)),
            pl.BlockSpec((tk, LANES), lambda bi, i: (i, 0)),
        ],
        out_specs=[
            pl.BlockSpec((1, 1, tk, QK_PAD), lambda bi, i: (bi, i, 0, 0)),
            pl.BlockSpec((1, 1, VT_ROWS, tk), lambda bi, i: (bi, i, 0, 0)),
        ],
        out_shape=[
            jax.ShapeDtypeStruct((b, nkt, tk, QK_PAD), BF16),
            jax.ShapeDtypeStruct((b, nkt, VT_ROWS, tk), BF16),
        ],
        compiler_params=_params(("arbitrary", "arbitrary")),
        name="kv",
    )(h, g.reshape(1, d), w_kv3, ckv_g.reshape(1, KV_LORA_RANK), cos_k, sin_k)


def _absorb_q_kernel(uk_ref, uq_ref, out_ref):
    out_ref[0] = lax.dot_general(uk_ref[0], uq_ref[0], (((1,), (1,)), ((), ())),
                                 precision=lax.Precision.HIGHEST,
                                 preferred_element_type=F32)


def _absorb_o_kernel(uv_ref, wo_ref, out_ref):
    out_ref[0] = jnp.dot(uv_ref[0], wo_ref[0], precision=lax.Precision.HIGHEST,
                         preferred_element_type=F32).astype(out_ref.dtype)


def _absorb_q(uk_h, uq_h):
    nh, lat, _ = uk_h.shape
    rank = uq_h.shape[1]
    return pl.pallas_call(
        _absorb_q_kernel,
        grid=(nh,),
        in_specs=[pl.BlockSpec((1,) + uk_h.shape[1:], lambda h: (h, 0, 0)),
                  pl.BlockSpec((1,) + uq_h.shape[1:], lambda h: (h, 0, 0))],
        out_specs=pl.BlockSpec((1, lat, rank), lambda h: (h, 0, 0)),
        out_shape=jax.ShapeDtypeStruct((nh, lat, rank), F32),
        compiler_params=_params(("arbitrary",)),
        name="absorb_q",
    )(uk_h, uq_h)


def _absorb_o(uv_h, wo_h):
    nh, lat, _ = uv_h.shape
    n = wo_h.shape[2]
    return pl.pallas_call(
        _absorb_o_kernel,
        grid=(nh,),
        in_specs=[pl.BlockSpec((1,) + uv_h.shape[1:], lambda h: (h, 0, 0)),
                  pl.BlockSpec((1,) + wo_h.shape[1:], lambda h: (h, 0, 0))],
        out_specs=pl.BlockSpec((1, lat, n), lambda h: (h, 0, 0)),
        out_shape=jax.ShapeDtypeStruct((nh, lat, n), BF16),
        compiler_params=_params(("arbitrary",)),
        name="absorb_o",
    )(uv_h, wo_h)


def _qproj_kernel(h_ref, g_ref, wdq_ref, qg_ref, wq_ref, cos_ref, sin_ref, qt_ref):
    u = _rmsnorm(h_ref[0], g_ref[...]).astype(BF16)
    scale = QK_DIM ** -0.5 * LOG2E
    cq = (_rmsnorm(jnp.dot(u, wdq_ref[...], preferred_element_type=F32),
                   qg_ref[...]) * scale).astype(BF16)
    cos = cos_ref[...]
    sin = sin_ref[...]
    tm = cq.shape[0]
    r0 = KV_LORA_RANK
    for hd in range(wq_ref.shape[0]):
        qt = lax.dot_general(wq_ref[hd], cq, (((1,), (1,)), ((), ())),
                             preferred_element_type=F32)
        x1 = qt[r0:r0 + HALF_ROPE]
        x2 = qt[r0 + HALF_ROPE:r0 + QK_ROPE_DIM]
        qt_ref[0, hd, 0:r0, :] = qt[0:r0].astype(BF16)
        qt_ref[0, hd, r0:r0 + HALF_ROPE, :] = (x1 * cos - x2 * sin).astype(BF16)
        qt_ref[0, hd, r0 + HALF_ROPE:r0 + QK_ROPE_DIM, :] = (x1 * sin + x2 * cos).astype(BF16)
        qt_ref[0, hd, r0 + QK_ROPE_DIM:, :] = jnp.zeros((QK_PAD - r0 - QK_ROPE_DIM, tm), BF16)


def _qproj(h, g, wdq, qg, wq_t, cos_t, sin_t, *, tm=QPROJ_TOKEN_TILE):
    b, s, d = h.shape
    nh = wq_t.shape[0]
    rank = wdq.shape[1]
    tm = min(tm, s)
    assert s % tm == 0, (s, tm)
    return pl.pallas_call(
        _qproj_kernel,
        grid=(b, s // tm),
        in_specs=[
            pl.BlockSpec((1, tm, d), lambda bi, i: (bi, i, 0)),
            _const_spec((1, d)),
            _const_spec(wdq.shape),
            _const_spec((1, rank)),
            _const_spec(wq_t.shape),
            pl.BlockSpec((HALF_ROPE, tm), lambda bi, i: (0, i)),
            pl.BlockSpec((HALF_ROPE, tm), lambda bi, i: (0, i)),
        ],
        out_specs=pl.BlockSpec((1, nh, QK_PAD, tm), lambda bi, i: (bi, 0, 0, i)),
        out_shape=jax.ShapeDtypeStruct((b, nh, QK_PAD, s), BF16),
        compiler_params=_params(("arbitrary", "arbitrary")),
        name="qproj",
    )(h, g.reshape(1, d), wdq, qg.reshape(1, rank), wq_t, cos_t, sin_t)


def _attn_kernel(qt_ref, k_ref, vt_ref, o_ref, acc_ref, m_ref, s_ref, cm_ref, *,
                 hb, tq, tk, qsub):
    g = pl.program_id(2)
    kd = (g * qsub * tq) // tk
    units = [(e, j) for e in range(qsub) for j in range(hb)]
    m_ref[...] = jnp.full(m_ref.shape, MASK_VALUE, F32)
    acc_ref[...] = jnp.zeros(acc_ref.shape, F32)

    def diag_rows(u):
        return (units[u][0] + 1) * tq

    def qk_unit(k, slot, u, rows=tk):
        e, j = units[u]
        s = jnp.dot(k[0:rows], qt_ref[0, j, :, e * tq:(e + 1) * tq],
                    preferred_element_type=F32)
        s_ref[slot, u, 0:rows] = s.astype(BF16)
        cm_ref[slot, u] = jnp.max(s, axis=0, keepdims=True)

    def softmax_pv_unit(vt, slot, u, keeps, rows=tk):
        s = s_ref[slot, u, 0:rows]
        if keeps is None:
            cm = cm_ref[slot, u]
        else:
            s = jnp.where(keeps[units[u][0]][0:rows], s, MASK_VALUE)
            cm = jnp.max(s, axis=0, keepdims=True).astype(F32)
        m_old = m_ref[u]
        m_new = jnp.maximum(m_old, cm).astype(BF16)
        m_ref[u] = m_new.astype(F32)
        alpha = jnp.exp2(m_old - m_new.astype(F32))
        p = jnp.exp2(s - m_new)
        pv = jnp.dot(vt[:, 0:rows], p, preferred_element_type=F32)
        acc_ref[u] = alpha * acc_ref[u] + pv

    def stage(kt_next, kt_cur, cur_slot, masked=False):
        k = None if kt_next is None else k_ref[0, kt_next]
        vt = vt_ref[0, kt_cur]
        keeps = None
        if masked:
            kpos = kt_cur * tk + lax.broadcasted_iota(jnp.int32, (tk, tq), 0)
            col = lax.broadcasted_iota(jnp.int32, (tk, tq), 1)
            keeps = [kpos <= (g * qsub + e) * tq + col for e in range(qsub)]
        for u in range(len(units)):
            if k is not None:
                qk_unit(k, 1 - cur_slot, u)
            softmax_pv_unit(vt, cur_slot, u, keeps, diag_rows(u) if masked else tk)

    k_diag = k_ref[0, kd]
    for u in range(len(units)):
        qk_unit(k_diag, 0, u, diag_rows(u))
    stage(jnp.maximum(kd - 1, 0), kd, 0, masked=True)

    def body(i, carry):
        idx = 2 * i + 1
        stage(kd - (idx + 1), kd - idx, 1)
        stage(kd - (idx + 2), kd - (idx + 1), 0)
        return carry

    lax.fori_loop(0, jnp.maximum(kd - 1, 0) // 2, body, 0)

    @pl.when(jnp.logical_and(lax.rem(kd, 2) == 0, kd >= 2))
    def _():
        stage(0, 1, 1)
        stage(None, 0, 0)

    @pl.when(lax.rem(kd, 2) == 1)
    def _():
        stage(None, 0, 1)

    for u, (e, j) in enumerate(units):
        o = acc_ref[u, 0:V_DIM, :] * (1.0 / acc_ref[u, V_DIM:V_DIM + 1, :])
        o_ref[0, e * tq:(e + 1) * tq, j * V_DIM:(j + 1) * V_DIM] = o.T.astype(o_ref.dtype)


def _attn(qt, k, vt, *, hb=ATTN_HEADS_PER_STEP, tq=ATTN_QUERY_TILE):
    b, nh, _, s = qt.shape
    nkt, tk = k.shape[1], k.shape[2]
    qsub = tk // tq
    assert tk == qsub * tq and vt.shape[2] == VT_ROWS
    assert s % tk == 0 and nh % hb == 0, (s, tk, nh, hb)
    n_units = hb * qsub
    return pl.pallas_call(
        functools.partial(_attn_kernel, hb=hb, tq=tq, tk=tk, qsub=qsub),
        grid=(b, nh // hb, s // tk),
        in_specs=[
            pl.BlockSpec((1, hb, QK_PAD, tk), lambda bi, hg, gi: (bi, hg, 0, gi)),
            pl.BlockSpec((1, nkt, tk, QK_PAD), lambda bi, hg, gi: (bi, 0, 0, 0),
                         pipeline_mode=pl.Buffered(1)),
            pl.BlockSpec((1, nkt, VT_ROWS, tk), lambda bi, hg, gi: (bi, 0, 0, 0),
                         pipeline_mode=pl.Buffered(1)),
        ],
        out_specs=pl.BlockSpec((1, tk, hb * V_DIM), lambda bi, hg, gi: (bi, gi, hg)),
        out_shape=jax.ShapeDtypeStruct((b, s, nh * V_DIM), BF16),
        scratch_shapes=[
            pltpu.VMEM((n_units, VT_ROWS, tq), F32),
            pltpu.VMEM((n_units, 1, tq), F32),
            pltpu.VMEM((2, n_units, tk, tq), BF16),
            pltpu.VMEM((2, n_units, 1, tq), F32),
        ],
        compiler_params=_params(("arbitrary", "arbitrary", "arbitrary")),
        name="attn",
    )(qt, k, vt)


def _rope_tables(seq):
    pos = jnp.arange(seq, dtype=F32)
    inv_freq = ROPE_THETA ** (-jnp.arange(0, QK_ROPE_DIM, 2, dtype=F32) / QK_ROPE_DIM)
    ang = pos[:, None] * inv_freq[None, :]
    return jnp.cos(ang), jnp.sin(ang)


def kernel(x, ffn_pre_norm, ffn_pre_wg, ffn_pre_wu, ffn_pre_wd, mix_norm, ffn_post_norm, ffn_post_wg, ffn_post_wu, ffn_post_wd, pool_w, pool_scale, kv_in_norm, w_dkv, ckv_norm, w_uk, w_uv, q_lora_norm, w_dq, w_uq, w_o, final_norm):
    b, s, d = x.shape
    depth = ffn_pre_norm.shape[0]
    n_a = pool_w.shape[0]
    t = b * s
    tq = min(ATTN_QUERY_TILE, s)
    tk = min(ATTN_KEY_TILE, s)

    pre = [w.astype(BF16) for w in (ffn_pre_wg, ffn_pre_wu, ffn_pre_wd)]
    post = [w.astype(BF16) for w in (ffn_post_wg, ffn_post_wu, ffn_post_wd)]
    pre_g = ffn_pre_norm.reshape(depth, 1, d)
    post_g = ffn_post_norm.reshape(depth, 1, d)
    pool_w_bf = pool_w.astype(BF16)

    cos, sin = _rope_tables(s)
    pad = jnp.zeros((s, LANES - QK_ROPE_DIM), F32)
    cos_k = jnp.concatenate([cos, cos, pad], axis=1)
    sin_k = jnp.concatenate([sin, sin, pad], axis=1)
    cos_t, sin_t = cos.T, sin.T

    wr = w_dkv[:, KV_LORA_RANK:]
    wpad = jnp.zeros((d, LANES - QK_ROPE_DIM), F32)
    w_kv3 = jnp.concatenate(
        [w_dkv[:, :KV_LORA_RANK], wr, wpad,
         -wr[:, HALF_ROPE:], wr[:, :HALF_ROPE], wpad], axis=1).astype(BF16)

    uk_h = jnp.transpose(w_uk, (1, 0, 2))
    uv_h = jnp.transpose(w_uv, (1, 0, 2))
    n_b = w_dq.shape[0]
    wq_ts, wo_abs = [], []
    for j in range(n_b):
        uq_h = jnp.transpose(w_uq[j], (1, 0, 2))
        a_nope = _absorb_q(uk_h, uq_h[:, :, :QK_NOPE_DIM])
        a_rope = jnp.transpose(uq_h[:, :, QK_NOPE_DIM:], (0, 2, 1))
        wq_ts.append(jnp.concatenate([a_nope, a_rope], axis=1).astype(BF16))
        wo_h = w_o[j].reshape(N_HEADS, V_DIM, d)
        wo_abs.append(_absorb_o(uv_h, wo_h).reshape(N_HEADS * KV_LORA_RANK, d))

    h = x.reshape(t, d)
    k_lat = vt_lat = None
    o_prev = wo_prev = None
    for l in range(depth):
        h = _ffn(h, l, pre_g, *pre)
        if l < n_a:
            h = _pool(h.reshape(b, s, d), mix_norm[l], pool_w_bf[l], pool_scale[l]).reshape(t, d)
            o_prev = wo_prev = None
        else:
            j = l - n_a
            qt = _qproj(h.reshape(b, s, d), mix_norm[l], w_dq[j].astype(BF16),
                        q_lora_norm[j], wq_ts[j], cos_t, sin_t)
            o_prev = _attn(qt, k_lat, vt_lat, tq=tq).reshape(t, N_HEADS * V_DIM)
            wo_prev = wo_abs[j]
        h = _ffn(h, l, post_g, *post, o2d=o_prev, wo=wo_prev,
                 final_g=final_norm if l == depth - 1 else None)
        if l == n_a - 1:
            k_lat, vt_lat = _kv(h.reshape(b, s, d), kv_in_norm, w_kv3, ckv_norm,
                                cos_k, sin_k, tk=tk)
    return h.reshape(b, s, d)
```

```python
import functools

import jax
import jax.numpy as jnp
from jax import lax
from jax.experimental import pallas as pl
from jax.experimental.pallas import tpu as pltpu

RMS_EPS = 1e-6
POOL_WINDOWS = (2, 4, 8, 16)
MAX_WINDOW = max(POOL_WINDOWS)
N_HEADS = 16
QK_NOPE_DIM = 128
QK_ROPE_DIM = 64
QK_DIM = QK_NOPE_DIM + QK_ROPE_DIM
V_DIM = 128
KV_LORA_RANK = 128
ROPE_THETA = 10000.0
HALF_ROPE = QK_ROPE_DIM // 2

VMEM_LIMIT_BYTES = 56 * 1024 * 1024
LANES = 128
QK_PAD = 256
BF16_SUBLANES = 16
VT_ROWS = KV_LORA_RANK + BF16_SUBLANES
LOG2E = 1.4426950408889634
MASK_VALUE = -1e30

FFN_TOKEN_TILE = 1024
FFN_HIDDEN_CHUNK = 256
POOL_TOKEN_TILE = 512
QPROJ_TOKEN_TILE = 1024
QPROJ_COLUMN_TILE = 256
ATTN_QUERY_TILE = 256
ATTN_KEY_TILE = 512
ATTN_HEADS_PER_STEP = 16

BF16 = jnp.bfloat16
F32 = jnp.float32


def _rmsnorm(x, g):
    ms = jnp.mean(x * x, axis=-1, keepdims=True)
    return x * lax.rsqrt(ms + RMS_EPS) * g


def _const_spec(shape):
    nd = len(shape)
    return pl.BlockSpec(shape, lambda *_: (0,) * nd, pipeline_mode=pl.Buffered(1))


def _layer_spec(stacked_shape, layer):
    rest = tuple(stacked_shape[1:])
    zeros = (0,) * len(rest)
    return pl.BlockSpec((None,) + rest, lambda *_: (layer,) + zeros,
                        pipeline_mode=pl.Buffered(1))


def _params(sem):
    return pltpu.CompilerParams(dimension_semantics=sem,
                                vmem_limit_bytes=VMEM_LIMIT_BYTES)


def _ffn_kernel(*refs, tf, with_oproj, with_final):
    it = iter(refs)
    h_ref = next(it)
    if with_oproj:
        o_ref = next(it)
        wo_ref = next(it)
    g_ref, wg_ref, wu_ref, wd_ref = next(it), next(it), next(it), next(it)
    if with_final:
        fg_ref = next(it)
    out_ref = next(it)
    a_ref = next(it)

    x = h_ref[...]
    if with_oproj:
        x = x + jnp.dot(o_ref[...], wo_ref[...], preferred_element_type=F32)
    u = _rmsnorm(x, g_ref[...]).astype(BF16)
    d_ff = wg_ref.shape[1]
    for c in range(d_ff // tf):
        sl = slice(c * tf, (c + 1) * tf)
        gate = jnp.dot(u, wg_ref[:, sl], preferred_element_type=F32)
        up = jnp.dot(u, wu_ref[:, sl], preferred_element_type=F32)
        a_ref[:, sl] = (gate * jax.nn.sigmoid(gate) * up).astype(BF16)
    y = jnp.dot(a_ref[...], wd_ref[...], preferred_element_type=F32)
    res = x + 0.5 * y
    if with_final:
        res = _rmsnorm(res, fg_ref[...])
    out_ref[...] = res


def _ffn(h2d, layer, g, wg, wu, wd, *, o2d=None, wo=None, final_g=None, tm=FFN_TOKEN_TILE, tf=FFN_HIDDEN_CHUNK):
    t, d = h2d.shape
    d_ff = wg.shape[2]
    tm = min(tm, t)
    assert t % tm == 0 and d_ff % tf == 0, (t, tm, d_ff, tf)
    row = lambda i: (i, 0)
    args = [h2d]
    specs = [pl.BlockSpec((tm, d), row)]
    if o2d is not None:
        args += [o2d, wo]
        specs += [pl.BlockSpec((tm, o2d.shape[1]), row), _const_spec(wo.shape)]
    args += [g, wg, wu, wd]
    specs += [_layer_spec(a.shape, layer) for a in (g, wg, wu, wd)]
    if final_g is not None:
        args.append(final_g.reshape(1, d))
        specs.append(_const_spec((1, d)))
    kern = functools.partial(_ffn_kernel, tf=tf, with_oproj=o2d is not None,
                             with_final=final_g is not None)
    return pl.pallas_call(
        kern,
        grid=(t // tm,),
        in_specs=specs,
        out_specs=pl.BlockSpec((tm, d), row),
        out_shape=jax.ShapeDtypeStruct((t, d), F32),
        scratch_shapes=[pltpu.VMEM((tm, d_ff), BF16)],
        compiler_params=_params(("arbitrary",)),
        name="ffn",
    )(*args)


def _pool_kernel(h_ref, halo_ref, g_ref, w_ref, sc_ref, out_ref, ext_ref, *, ts):
    i = pl.program_id(1)
    g = g_ref[...]
    x = h_ref[0]
    u = _rmsnorm(x, g)
    halo = _rmsnorm(halo_ref[0], g)
    ext_ref[0:MAX_WINDOW, :] = jnp.where(i > 0, halo, 0.0)
    ext_ref[MAX_WINDOW:, :] = u
    pos = i * ts + lax.broadcasted_iota(jnp.int32, (ts, 1), 0)
    dg = w_ref.shape[1]
    for gi, w in enumerate(POOL_WINDOWS):
        cs = slice(gi * dg, (gi + 1) * dg)
        win = ext_ref[:, cs]
        shift = 1
        while shift < w:
            win = win + pltpu.roll(win, shift, 0)
            shift *= 2
        win = win[MAX_WINDOW:]
        ug = u[:, cs]
        inv_count = 1.0 / jnp.minimum(pos + 1, w).astype(F32)
        y = (win * inv_count - ug).astype(BF16)
        z = jnp.dot(y, w_ref[gi], preferred_element_type=F32)
        out_ref[0, :, cs] = x[:, cs] + z * sc_ref[:, cs]


def _pool(h, g, w_groups, scale, *, ts=POOL_TOKEN_TILE):
    b, s, d = h.shape
    ts = min(ts, s)
    assert all(w & (w - 1) == 0 for w in POOL_WINDOWS)
    assert s % ts == 0 and ts % MAX_WINDOW == 0 and len(POOL_WINDOWS) == w_groups.shape[0]
    halo_blocks = ts // MAX_WINDOW
    return pl.pallas_call(
        functools.partial(_pool_kernel, ts=ts),
        grid=(b, s // ts),
        in_specs=[
            pl.BlockSpec((1, ts, d), lambda bi, i: (bi, i, 0)),
            pl.BlockSpec((1, MAX_WINDOW, d),
                         lambda bi, i: (bi, jnp.maximum(i * halo_blocks - 1, 0), 0)),
            _const_spec((1, d)),
            _const_spec(w_groups.shape),
            _const_spec((1, d)),
        ],
        out_specs=pl.BlockSpec((1, ts, d), lambda bi, i: (bi, i, 0)),
        out_shape=jax.ShapeDtypeStruct((b, s, d), F32),
        scratch_shapes=[pltpu.VMEM((ts + MAX_WINDOW, d), F32)],
        compiler_params=_params(("arbitrary", "arbitrary")),
        name="pool",
    )(h, h, g.reshape(1, d), w_groups, scale.reshape(1, d))


def _kv_kernel(h_ref, g_ref, w_ref, cg_ref, cos_ref, sin_ref, k_ref, vt_ref):
    u = _rmsnorm(h_ref[0], g_ref[...]).astype(BF16)
    kv = jnp.dot(u, w_ref[...], preferred_element_type=F32)
    c = _rmsnorm(kv[:, 0:LANES], cg_ref[...])
    rope = kv[:, LANES:2 * LANES] * cos_ref[...] + kv[:, 2 * LANES:] * sin_ref[...]
    k_ref[0, 0, :, 0:LANES] = c.astype(BF16)
    k_ref[0, 0, :, LANES:] = rope.astype(BF16)
    vt_ref[0, 0, 0:KV_LORA_RANK, :] = c.T.astype(BF16)
    extra = (BF16_SUBLANES, c.shape[0])
    ones_row = lax.broadcasted_iota(jnp.int32, extra, 0) == 0
    vt_ref[0, 0, KV_LORA_RANK:, :] = jnp.where(ones_row, 1.0, 0.0).astype(BF16)


def _kv(h, g, w_kv3, ckv_g, cos_k, sin_k, *, tk):
    b, s, d = h.shape
    assert s % tk == 0, (s, tk)
    nkt = s // tk
    return pl.pallas_call(
        _kv_kernel,
        grid=(b, nkt),
        in_specs=[
            pl.BlockSpec((1, tk, d), lambda bi, i: (bi, i, 0)),
            _const_spec((1, d)),
            _const_spec(w_kv3.shape),
            _const_spec((1, KV_LORA_RANK)),
            pl.BlockSpec((tk, LANES), lambda bi, i: (i, 0)),
            pl.BlockSpec((tk, LANES), lambda bi, i: (i, 0)),
        ],
        out_specs=[
            pl.BlockSpec((1, 1, tk, QK_PAD), lambda bi, i: (bi, i, 0, 0)),
            pl.BlockSpec((1, 1, VT_ROWS, tk), lambda bi, i: (bi, i, 0, 0)),
        ],
        out_shape=[
            jax.ShapeDtypeStruct((b, nkt, tk, QK_PAD), BF16),
            jax.ShapeDtypeStruct((b, nkt, VT_ROWS, tk), BF16),
        ],
        compiler_params=_params(("arbitrary", "arbitrary")),
        name="kv",
    )(h, g.reshape(1, d), w_kv3, ckv_g.reshape(1, KV_LORA_RANK), cos_k, sin_k)


def _absorb_q_kernel(uk_ref, uq_ref, out_ref):
    out_ref[0] = lax.dot_general(uk_ref[0], uq_ref[0], (((1,), (1,)), ((), ())),
                                 precision=lax.Precision.HIGHEST,
                                 preferred_element_type=F32)


def _absorb_o_kernel(uv_ref, wo_ref, out_ref):
    out_ref[0] = jnp.dot(uv_ref[0], wo_ref[0], precision=lax.Precision.HIGHEST,
                         preferred_element_type=F32).astype(out_ref.dtype)


def _absorb_q(uk_h, uq_h):
    nh, lat, _ = uk_h.shape
    rank = uq_h.shape[1]
    return pl.pallas_call(
        _absorb_q_kernel,
        grid=(nh,),
        in_specs=[pl.BlockSpec((1,) + uk_h.shape[1:], lambda h: (h, 0, 0)),
                  pl.BlockSpec((1,) + uq_h.shape[1:], lambda h: (h, 0, 0))],
        out_specs=pl.BlockSpec((1, lat, rank), lambda h: (h, 0, 0)),
        out_shape=jax.ShapeDtypeStruct((nh, lat, rank), F32),
        compiler_params=_params(("arbitrary",)),
        name="absorb_q",
    )(uk_h, uq_h)


def _absorb_o(uv_h, wo_h):
    nh, lat, _ = uv_h.shape
    n = wo_h.shape[2]
    return pl.pallas_call(
        _absorb_o_kernel,
        grid=(nh,),
        in_specs=[pl.BlockSpec((1,) + uv_h.shape[1:], lambda h: (h, 0, 0)),
                  pl.BlockSpec((1,) + wo_h.shape[1:], lambda h: (h, 0, 0))],
        out_specs=pl.BlockSpec((1, lat, n), lambda h: (h, 0, 0)),
        out_shape=jax.ShapeDtypeStruct((nh, lat, n), BF16),
        compiler_params=_params(("arbitrary",)),
        name="absorb_o",
    )(uv_h, wo_h)


def _qproj_kernel(h_ref, g_ref, wdq_ref, qg_ref, wq_ref, cos_ref, sin_ref, qt_ref):
    u = _rmsnorm(h_ref[0], g_ref[...]).astype(BF16)
    scale = QK_DIM ** -0.5 * LOG2E
    cq = (_rmsnorm(jnp.dot(u, wdq_ref[...], preferred_element_type=F32),
                   qg_ref[...]) * scale).astype(BF16)
    tm = cq.shape[0]
    r0 = KV_LORA_RANK
    nh = qt_ref.shape[1]
    ct = min(QPROJ_COLUMN_TILE, tm)
    for c in range(tm // ct):
        cols = slice(c * ct, (c + 1) * ct)
        cos = cos_ref[:, cols]
        sin = sin_ref[:, cols]
        qt_all = lax.dot_general(wq_ref[...], cq[cols], (((1,), (1,)), ((), ())),
                                 preferred_element_type=F32)
        for hd in range(nh):
            qt = qt_all[hd * QK_DIM:(hd + 1) * QK_DIM]
            x1 = qt[r0:r0 + HALF_ROPE]
            x2 = qt[r0 + HALF_ROPE:r0 + QK_ROPE_DIM]
            qt_ref[0, hd, 0:r0, cols] = qt[0:r0].astype(BF16)
            qt_ref[0, hd, r0:r0 + HALF_ROPE, cols] = (x1 * cos - x2 * sin).astype(BF16)
            qt_ref[0, hd, r0 + HALF_ROPE:r0 + QK_ROPE_DIM, cols] = (x1 * sin + x2 * cos).astype(BF16)
            qt_ref[0, hd, r0 + QK_ROPE_DIM:, cols] = jnp.zeros((QK_PAD - r0 - QK_ROPE_DIM, ct), BF16)


def _qproj(h, g, wdq, qg, wq_t, cos_t, sin_t, *, tm=QPROJ_TOKEN_TILE):
    b, s, d = h.shape
    nh = wq_t.shape[0] // QK_DIM
    rank = wdq.shape[1]
    tm = min(tm, s)
    assert s % tm == 0, (s, tm)
    return pl.pallas_call(
        _qproj_kernel,
        grid=(b, s // tm),
        in_specs=[
            pl.BlockSpec((1, tm, d), lambda bi, i: (bi, i, 0)),
            _const_spec((1, d)),
            _const_spec(wdq.shape),
            _const_spec((1, rank)),
            _const_spec(wq_t.shape),
            pl.BlockSpec((HALF_ROPE, tm), lambda bi, i: (0, i)),
            pl.BlockSpec((HALF_ROPE, tm), lambda bi, i: (0, i)),
        ],
        out_specs=pl.BlockSpec((1, nh, QK_PAD, tm), lambda bi, i: (bi, 0, 0, i)),
        out_shape=jax.ShapeDtypeStruct((b, nh, QK_PAD, s), BF16),
        compiler_params=_params(("arbitrary", "arbitrary")),
        name="qproj",
    )(h, g.reshape(1, d), wdq, qg.reshape(1, rank), wq_t, cos_t, sin_t)


def _attn_kernel(qt_ref, k_ref, vt_ref, o_ref, acc_ref, m_ref, s_ref, cm_ref, *,
                 hb, tq, tk, qsub):
    g = pl.program_id(2)
    kd = (g * qsub * tq) // tk
    units = [(e, j) for e in range(qsub) for j in range(hb)]
    m_ref[...] = jnp.full(m_ref.shape, MASK_VALUE, F32)
    acc_ref[...] = jnp.zeros(acc_ref.shape, F32)

    def diag_rows(u):
        return (units[u][0] + 1) * tq

    def qk_unit(k, slot, u, rows=tk):
        e, j = units[u]
        s = jnp.dot(k[0:rows], qt_ref[0, j, :, e * tq:(e + 1) * tq],
                    preferred_element_type=F32)
        s_ref[slot, u, 0:rows] = s.astype(BF16)
        cm_ref[slot, u] = jnp.max(s, axis=0, keepdims=True)

    def softmax_pv_unit(vt, slot, u, keeps, rows=tk):
        s = s_ref[slot, u, 0:rows]
        if keeps is None:
            cm = cm_ref[slot, u]
        else:
            s = jnp.where(keeps[units[u][0]][0:rows], s, MASK_VALUE)
            cm = jnp.max(s, axis=0, keepdims=True).astype(F32)
        m_old = m_ref[u]
        m_new = jnp.maximum(m_old, cm).astype(BF16)
        m_ref[u] = m_new.astype(F32)
        alpha = jnp.exp2(m_old - m_new.astype(F32))
        p = jnp.exp2(s - m_new)
        pv = jnp.dot(vt[:, 0:rows], p, preferred_element_type=F32)
        acc_ref[u] = alpha * acc_ref[u] + pv

    def stage(kt_next, kt_cur, cur_slot, masked=False):
        k = None if kt_next is None else k_ref[0, kt_next]
        vt = vt_ref[0, kt_cur]
        keeps = None
        if masked:
            kpos = kt_cur * tk + lax.broadcasted_iota(jnp.int32, (tk, tq), 0)
            col = lax.broadcasted_iota(jnp.int32, (tk, tq), 1)
            keeps = [kpos <= (g * qsub + e) * tq + col for e in range(qsub)]
        for u in range(len(units)):
            if k is not None:
                qk_unit(k, 1 - cur_slot, u)
            softmax_pv_unit(vt, cur_slot, u, keeps, diag_rows(u) if masked else tk)

    k_diag = k_ref[0, kd]
    for u in range(len(units)):
        qk_unit(k_diag, 0, u, diag_rows(u))
    stage(jnp.maximum(kd - 1, 0), kd, 0, masked=True)

    def body(i, carry):
        idx = 2 * i + 1
        stage(kd - (idx + 1), kd - idx, 1)
        stage(kd - (idx + 2), kd - (idx + 1), 0)
        return carry

    lax.fori_loop(0, jnp.maximum(kd - 1, 0) // 2, body, 0)

    @pl.when(jnp.logical_and(lax.rem(kd, 2) == 0, kd >= 2))
    def _():
        stage(0, 1, 1)
        stage(None, 0, 0)

    @pl.when(lax.rem(kd, 2) == 1)
    def _():
        stage(None, 0, 1)

    for u, (e, j) in enumerate(units):
        o = acc_ref[u, 0:V_DIM, :] * (1.0 / acc_ref[u, V_DIM:V_DIM + 1, :])
        o_ref[0, e * tq:(e + 1) * tq, j * V_DIM:(j + 1) * V_DIM] = o.T.astype(o_ref.dtype)


def _attn(qt, k, vt, *, hb=ATTN_HEADS_PER_STEP, tq=ATTN_QUERY_TILE):
    b, nh, _, s = qt.shape
    nkt, tk = k.shape[1], k.shape[2]
    qsub = tk // tq
    assert tk == qsub * tq and vt.shape[2] == VT_ROWS
    assert s % tk == 0 and nh % hb == 0, (s, tk, nh, hb)
    n_units = hb * qsub
    return pl.pallas_call(
        functools.partial(_attn_kernel, hb=hb, tq=tq, tk=tk, qsub=qsub),
        grid=(b, nh // hb, s // tk),
        in_specs=[
            pl.BlockSpec((1, hb, QK_PAD, tk), lambda bi, hg, gi: (bi, hg, 0, gi)),
            pl.BlockSpec((1, nkt, tk, QK_PAD), lambda bi, hg, gi: (bi, 0, 0, 0),
                         pipeline_mode=pl.Buffered(1)),
            pl.BlockSpec((1, nkt, VT_ROWS, tk), lambda bi, hg, gi: (bi, 0, 0, 0),
                         pipeline_mode=pl.Buffered(1)),
        ],
        out_specs=pl.BlockSpec((1, tk, hb * V_DIM), lambda bi, hg, gi: (bi, gi, hg)),
        out_shape=jax.ShapeDtypeStruct((b, s, nh * V_DIM), BF16),
        scratch_shapes=[
            pltpu.VMEM((n_units, VT_ROWS, tq), F32),
            pltpu.VMEM((n_units, 1, tq), F32),
            pltpu.VMEM((2, n_units, tk, tq), BF16),
            pltpu.VMEM((2, n_units, 1, tq), F32),
        ],
        compiler_params=_params(("arbitrary", "arbitrary", "arbitrary")),
        name="attn",
    )(qt, k, vt)


def _rope_tables(seq):
    pos = jnp.arange(seq, dtype=F32)
    inv_freq = ROPE_THETA ** (-jnp.arange(0, QK_ROPE_DIM, 2, dtype=F32) / QK_ROPE_DIM)
    ang = pos[:, None] * inv_freq[None, :]
    return jnp.cos(ang), jnp.sin(ang)


def kernel(x, ffn_pre_norm, ffn_pre_wg, ffn_pre_wu, ffn_pre_wd, mix_norm, ffn_post_norm, ffn_post_wg, ffn_post_wu, ffn_post_wd, pool_w, pool_scale, kv_in_norm, w_dkv, ckv_norm, w_uk, w_uv, q_lora_norm, w_dq, w_uq, w_o, final_norm):
    b, s, d = x.shape
    depth = ffn_pre_norm.shape[0]
    n_a = pool_w.shape[0]
    t = b * s
    tq = min(ATTN_QUERY_TILE, s)
    tk = min(ATTN_KEY_TILE, s)

    pre = [w.astype(BF16) for w in (ffn_pre_wg, ffn_pre_wu, ffn_pre_wd)]
    post = [w.astype(BF16) for w in (ffn_post_wg, ffn_post_wu, ffn_post_wd)]
    pre_g = ffn_pre_norm.reshape(depth, 1, d)
    post_g = ffn_post_norm.reshape(depth, 1, d)
    pool_w_bf = pool_w.astype(BF16)

    cos, sin = _rope_tables(s)
    pad = jnp.zeros((s, LANES - QK_ROPE_DIM), F32)
    cos_k = jnp.concatenate([cos, cos, pad], axis=1)
    sin_k = jnp.concatenate([sin, sin, pad], axis=1)
    cos_t, sin_t = cos.T, sin.T

    wr = w_dkv[:, KV_LORA_RANK:]
    wpad = jnp.zeros((d, LANES - QK_ROPE_DIM), F32)
    w_kv3 = jnp.concatenate(
        [w_dkv[:, :KV_LORA_RANK], wr, wpad,
         -wr[:, HALF_ROPE:], wr[:, :HALF_ROPE], wpad], axis=1).astype(BF16)

    uk_h = jnp.transpose(w_uk, (1, 0, 2))
    uv_h = jnp.transpose(w_uv, (1, 0, 2))
    n_b = w_dq.shape[0]
    wq_ts, wo_abs = [], []
    for j in range(n_b):
        uq_h = jnp.transpose(w_uq[j], (1, 0, 2))
        a_nope = _absorb_q(uk_h, uq_h[:, :, :QK_NOPE_DIM])
        a_rope = jnp.transpose(uq_h[:, :, QK_NOPE_DIM:], (0, 2, 1))
        wq_ts.append(jnp.concatenate([a_nope, a_rope], axis=1).astype(BF16)
                     .reshape(N_HEADS * QK_DIM, -1))
        wo_h = w_o[j].reshape(N_HEADS, V_DIM, d)
        wo_abs.append(_absorb_o(uv_h, wo_h).reshape(N_HEADS * KV_LORA_RANK, d))

    h = x.reshape(t, d)
    k_lat = vt_lat = None
    o_prev = wo_prev = None
    for l in range(depth):
        h = _ffn(h, l, pre_g, *pre)
        if l < n_a:
            h = _pool(h.reshape(b, s, d), mix_norm[l], pool_w_bf[l], pool_scale[l]).reshape(t, d)
            o_prev = wo_prev = None
        else:
            j = l - n_a
            qt = _qproj(h.reshape(b, s, d), mix_norm[l], w_dq[j].astype(BF16),
                        q_lora_norm[j], wq_ts[j], cos_t, sin_t)
            o_prev = _attn(qt, k_lat, vt_lat, tq=tq).reshape(t, N_HEADS * V_DIM)
            wo_prev = wo_abs[j]
        h = _ffn(h, l, post_g, *post, o2d=o_prev, wo=wo_prev,
                 final_g=final_norm if l == depth - 1 else None)
        if l == n_a - 1:
            k_lat, vt_lat = _kv(h.reshape(b, s, d), kv_in_norm, w_kv3, ckv_norm,
                                cos_k, sin_k, tk=tk)
    return h.reshape(b, s, d)
```
